```python
import math
import jax, jax.numpy as jnp
from jax import lax
import numpy as np

D_MODEL = 1024
BATCH = 4
SEQ = 4096
DEPTH = 2

N_BRANCH = 4
MIX_WIDTH = D_MODEL // N_BRANCH
EPS = 1e-6
Q_BLOCK = 128

A_HEAD_DIM = 64
A_HEADS = MIX_WIDTH // A_HEAD_DIM
IDX_HEADS = 8
IDX_DIM = 32
TOPK_MAX = 256

POOL_WINDOWS = (2, 4, 8, 16)
POOL_GROUPS = len(POOL_WINDOWS)
POOL_GROUP_DIM = MIX_WIDTH // POOL_GROUPS

GMLP_CHUNK = 128
GMLP_GROUPS = 4
GMLP_GROUP_DIM = MIX_WIDTH // GMLP_GROUPS

SB_HEAD_DIM = 64
SB_HEADS = MIX_WIDTH // SB_HEAD_DIM

D_FF = ((8 * D_MODEL + 3 * 256 - 1) // (3 * 256)) * 256

SPLIT_SIZES = (
    MIX_WIDTH, MIX_WIDTH, MIX_WIDTH,
    IDX_HEADS * IDX_DIM,
    IDX_DIM,
    IDX_HEADS,
    MIX_WIDTH,
    MIX_WIDTH, MIX_WIDTH,
    MIX_WIDTH, MIX_WIDTH, MIX_WIDTH,
    N_BRANCH * D_MODEL,
)
D_IN = sum(SPLIT_SIZES)

kernel_name = "hybrid_gated_dsa_pool_gmlp_stickbreak"


def _split_points():
    return [int(c) for c in np.cumsum(np.array(SPLIT_SIZES))[:-1]]


def rms_norm(x, g):
    xf = x.astype(jnp.float32)
    y = xf * lax.rsqrt(jnp.mean(xf * xf, axis=-1, keepdims=True) + EPS)
    return (y * g.astype(jnp.float32)).astype(x.dtype)


def layer_norm(x, g):
    xf = x.astype(jnp.float32)
    mu = jnp.mean(xf, axis=-1, keepdims=True)
    xc = xf - mu
    y = xc * lax.rsqrt(jnp.mean(xc * xc, axis=-1, keepdims=True) + EPS)
    return (y * g.astype(jnp.float32)).astype(x.dtype)


def sparse_indexed_attention(q, k, v, q_idx, k_idx, w_idx, q_norm, k_norm):
    B, S, H, Dh = q.shape
    topk = min(TOPK_MAX, S // 4)
    q = rms_norm(q, q_norm)
    k = rms_norm(k, k_norm)
    scale = Dh ** -0.5
    n_blocks = S // Q_BLOCK
    s_pos = jnp.arange(S)
    k_idx_f = k_idx.astype(jnp.float32)

    def block(i):
        start = i * Q_BLOCK
        qb = lax.dynamic_slice_in_dim(q, start, Q_BLOCK, axis=1)
        qib = lax.dynamic_slice_in_dim(q_idx, start, Q_BLOCK, axis=1).astype(jnp.float32)
        wb = lax.dynamic_slice_in_dim(w_idx, start, Q_BLOCK, axis=1).astype(jnp.float32)
        t_pos = start + jnp.arange(Q_BLOCK)
        head_scores = jax.nn.relu(jnp.einsum('bthd,bsd->bths', qib, k_idx_f))
        idx_score = jnp.einsum('bth,bths->bts', wb, head_scores)
        causal = s_pos[None, :] <= t_pos[:, None]
        idx_score = jnp.where(causal[None], idx_score, -jnp.inf)
        _, sel = lax.top_k(idx_score, topk)
        valid = sel <= t_pos[None, :, None]
        k_sel = jax.vmap(lambda kk, ii: kk[ii])(k, sel)
        v_sel = jax.vmap(lambda vv, ii: vv[ii])(v, sel)
        logits = jnp.einsum('bthd,btkhd->bhtk', qb, k_sel).astype(jnp.float32) * scale
        logits = jnp.where(valid[:, None], logits, -jnp.inf)
        p = jax.nn.softmax(logits, axis=-1).astype(v.dtype)
        return jnp.einsum('bhtk,btkhd->bthd', p, v_sel)

    out = lax.map(block, jnp.arange(n_blocks))
    return out.transpose(1, 0, 2, 3, 4).reshape(B, S, H * Dh)


def multiscale_pool(p, w_grp, scale):
    B, S, _ = p.shape
    pg = p.reshape(B, S, POOL_GROUPS, POOL_GROUP_DIM).astype(jnp.float32)
    csum = jnp.pad(jnp.cumsum(pg, axis=1), ((0, 0), (1, 0), (0, 0), (0, 0)))
    pos = jnp.arange(S)
    means = []
    for g, w in enumerate(POOL_WINDOWS):
        lo_idx = jnp.maximum(pos + 1 - w, 0)
        cnt = jnp.minimum(pos + 1, w).astype(jnp.float32)
        means.append((csum[:, 1:, g] - csum[:, lo_idx, g]) / cnt[None, :, None])
    pooled = jnp.stack(means, axis=2) - pg
    y = jnp.einsum('bsgc,gcd->bsgd', pooled, w_grp.astype(jnp.float32))
    return (y.reshape(B, S, MIX_WIDTH) * scale.astype(jnp.float32)).astype(p.dtype)


def chunked_spatial_gating(u, v, ln_g, w_s, b_s):
    B, S, _ = v.shape
    v = layer_norm(v, ln_g)
    vg = v.reshape(B, S // GMLP_CHUNK, GMLP_CHUNK, GMLP_GROUPS, GMLP_GROUP_DIM)
    mask = jnp.tril(jnp.ones((GMLP_CHUNK, GMLP_CHUNK), dtype=w_s.dtype))
    mixed = jnp.einsum('gts,bnsgd->bntgd', w_s * mask[None], vg)
    mixed = mixed + b_s.T[None, None, :, :, None]
    return u * mixed.reshape(B, S, MIX_WIDTH)


def stick_breaking_attention(q, k, v):
    B, S, H, Dh = q.shape
    scale = Dh ** -0.5
    n_blocks = S // Q_BLOCK
    s_pos = jnp.arange(S)

    def block(i):
        start = i * Q_BLOCK
        qb = lax.dynamic_slice_in_dim(q, start, Q_BLOCK, axis=1)
        t_pos = start + jnp.arange(Q_BLOCK)
        z = jnp.einsum('bthd,bshd->bhts', qb, k).astype(jnp.float32) * scale
        strict = (s_pos[None, :] < t_pos[:, None])[None, None]
        log_1m = jnp.where(strict, jax.nn.log_sigmoid(-z), 0.0)
        tail = lax.cumsum(log_1m, axis=3, reverse=True) - log_1m
        a = jnp.where(strict, jnp.exp(jax.nn.log_sigmoid(z) + tail), 0.0)
        return jnp.einsum('bhts,bshd->bthd', a.astype(v.dtype), v)

    out = lax.map(block, jnp.arange(n_blocks))
    return out.transpose(1, 0, 2, 3, 4).reshape(B, S, H * Dh)


def hybrid_mixer(h, w_in, q_norm, k_norm, pool_w, pool_scale,
                 gm_norm, gm_ws, gm_b, w_branch, w_out):
    B, S, _ = h.shape
    z = h @ w_in
    (aq, ak, av, iq, ik, iw, pin, gu, gv, sq, sk, sv, gl) = jnp.split(z, _split_points(), axis=-1)
    o_a = sparse_indexed_attention(
        aq.reshape(B, S, A_HEADS, A_HEAD_DIM), ak.reshape(B, S, A_HEADS, A_HEAD_DIM),
        av.reshape(B, S, A_HEADS, A_HEAD_DIM), iq.reshape(B, S, IDX_HEADS, IDX_DIM),
        ik, iw, q_norm, k_norm)
    o_b = multiscale_pool(pin, pool_w, pool_scale)
    o_c = chunked_spatial_gating(jax.nn.gelu(gu), jax.nn.gelu(gv), gm_norm, gm_ws, gm_b)
    o_d = stick_breaking_attention(
        sq.reshape(B, S, SB_HEADS, SB_HEAD_DIM), sk.reshape(B, S, SB_HEADS, SB_HEAD_DIM),
        sv.reshape(B, S, SB_HEADS, SB_HEAD_DIM))
    o = jnp.stack([o_a, o_b, o_c, o_d], axis=2)
    proj = jnp.einsum('bsnw,nwd->bsnd', o, w_branch)
    gate = jax.nn.sigmoid(gl.astype(jnp.float32)).reshape(B, S, N_BRANCH, D_MODEL)
    merged = jnp.sum(gate.astype(proj.dtype) * proj, axis=2)
    return merged @ w_out


def swiglu_ffn(h, w_gate, w_up, w_down):
    return (jax.nn.silu(h @ w_gate) * (h @ w_up)) @ w_down


def setup_inputs(seed: int = 0) -> dict:
    key = jax.random.key(seed)
    ks = jax.random.split(key, 17)
    f32 = jnp.float32

    def nrm(k, shape, s):
        return jax.random.normal(k, shape, f32) * s

    return {
        "x": nrm(ks[0], (BATCH, SEQ, D_MODEL), 1.0),
        "mix_norm": 1.0 + nrm(ks[1], (DEPTH, D_MODEL), 0.05),
        "w_in": nrm(ks[2], (DEPTH, D_MODEL, D_IN), D_MODEL ** -0.5),
        "attn_q_norm": 1.0 + nrm(ks[3], (DEPTH, A_HEAD_DIM), 0.05),
        "attn_k_norm": 1.0 + nrm(ks[4], (DEPTH, A_HEAD_DIM), 0.05),
        "pool_w": nrm(ks[5], (DEPTH, POOL_GROUPS, POOL_GROUP_DIM, POOL_GROUP_DIM), POOL_GROUP_DIM ** -0.5),
        "pool_scale": 1.0 + nrm(ks[6], (DEPTH, MIX_WIDTH), 0.05),
        "gmlp_norm": 1.0 + nrm(ks[7], (DEPTH, MIX_WIDTH), 0.05),
        "gmlp_w_s": nrm(ks[8], (DEPTH, GMLP_GROUPS, GMLP_CHUNK, GMLP_CHUNK), GMLP_CHUNK ** -0.5),
        "gmlp_b": 1.0 + nrm(ks[9], (DEPTH, GMLP_GROUPS, GMLP_CHUNK), 0.1),
        "w_branch": nrm(ks[10], (DEPTH, N_BRANCH, MIX_WIDTH, D_MODEL), MIX_WIDTH ** -0.5),
        "w_out": nrm(ks[11], (DEPTH, D_MODEL, D_MODEL), D_MODEL ** -0.5),
        "ffn_norm": 1.0 + nrm(ks[12], (DEPTH, D_MODEL), 0.05),
        "w_ffn_gate": nrm(ks[13], (DEPTH, D_MODEL, D_FF), D_MODEL ** -0.5),
        "w_ffn_up": nrm(ks[14], (DEPTH, D_MODEL, D_FF), D_MODEL ** -0.5),
        "w_ffn_down": nrm(ks[15], (DEPTH, D_FF, D_MODEL), D_FF ** -0.5),
    }


def reference(x, mix_norm, w_in, attn_q_norm, attn_k_norm, pool_w, pool_scale,
              gmlp_norm, gmlp_w_s, gmlp_b, w_branch, w_out, ffn_norm,
              w_ffn_gate, w_ffn_up, w_ffn_down):
    for l in range(DEPTH):
        h = rms_norm(x, mix_norm[l])
        x = x + hybrid_mixer(h, w_in[l], attn_q_norm[l], attn_k_norm[l], pool_w[l],
                             pool_scale[l], gmlp_norm[l], gmlp_w_s[l], gmlp_b[l],
                             w_branch[l], w_out[l])
        h = rms_norm(x, ffn_norm[l])
        x = x + swiglu_ffn(h, w_ffn_gate[l], w_ffn_up[l], w_ffn_down[l])
    return x
```

```python
import functools

import jax
import jax.numpy as jnp
from jax import lax
from jax.experimental import pallas as pl
from jax.experimental.pallas import tpu as pltpu

F32 = jnp.float32
BF16 = jnp.bfloat16

EPS = 1e-6
N_BRANCH = 4
MIX_WIDTH = 256
HEAD_DIM = 64
N_HEADS = MIX_WIDTH // HEAD_DIM
IDX_HEADS = 8
IDX_DIM = 32
TOPK_MAX = 256
POOL_WINDOWS = (2, 4, 8, 16)
POOL_HALO = 16
GMLP_CHUNK = 128
GMLP_GROUPS = 4
ATTN_SCALE = HEAD_DIM ** -0.5

SEG = 256
N_SEGS = 11
LANES = 128

INPROJ_TM = 1024
INPROJ_TN = 768
ATTN_T = 256
MERGE_TM = 512
FFN_TM = 512
FFN_TF = 1408

MASK_BIAS = -2e30
MAX_INIT = -1e30
BISECT_MAX_ITERS = 64

VMEM_LIMIT = 56 * 1024 * 1024


def _cparams(sem):
    return pltpu.CompilerParams(dimension_semantics=sem, vmem_limit_bytes=VMEM_LIMIT)


def _dot(a, b):
    return jnp.dot(a, b, preferred_element_type=F32)


def _dot_nt(a, b):
    return lax.dot_general(a, b, (((1,), (1,)), ((), ())), preferred_element_type=F32)


def _sigmoid(x):
    return 1.0 / (1.0 + jnp.exp(-x))


def _gelu_tanh(x):
    c = 0.7978845608028654
    return 0.5 * x * (1.0 + jnp.tanh(c * (x + 0.044715 * (x * x * x))))


def _split_dot(x, ones_mat):
    hi = x.astype(BF16)
    lo = (x - hi.astype(F32)).astype(BF16)
    return _dot(hi, ones_mat) + _dot(lo, ones_mat)


def _head_blockdiag_ones():
    r = lax.broadcasted_iota(jnp.int32, (MIX_WIDTH, MIX_WIDTH), 0) // HEAD_DIM
    c = lax.broadcasted_iota(jnp.int32, (MIX_WIDTH, MIX_WIDTH), 1) // HEAD_DIM
    return jnp.where(r == c, 1.0, 0.0).astype(BF16)


def _head_rms_norm(x, gain):
    ms = _split_dot(x * x, _head_blockdiag_ones()) * (1.0 / HEAD_DIM)
    return x * lax.rsqrt(ms + EPS) * gain


def _lane_head(shape):
    return lax.broadcasted_iota(jnp.int32, shape, len(shape) - 1) // HEAD_DIM


def _rep2(a):
    return jnp.concatenate([a, a], axis=1)


def _inproj_kernel(x_ref, g_ref, w_ref, o_ref, h_scr):
    @pl.when(pl.program_id(1) == 0)
    def _():
        x = x_ref[...]
        ms = jnp.mean(x * x, axis=-1, keepdims=True)
        h_scr[...] = (x * lax.rsqrt(ms + EPS) * g_ref[...]).astype(BF16)

    o_ref[...] = _dot(h_scr[...], w_ref[...])


def _inproj(x2, gain, w):
    n, d = x2.shape
    d_in = w.shape[1]
    return pl.pallas_call(
        _inproj_kernel,
        out_shape=jax.ShapeDtypeStruct((n, d_in), F32),
        grid=(n // INPROJ_TM, d_in // INPROJ_TN),
        in_specs=[
            pl.BlockSpec((INPROJ_TM, d), lambda i, j: (i, 0)),
            pl.BlockSpec((1, d), lambda i, j: (0, 0)),
            pl.BlockSpec((d, INPROJ_TN), lambda i, j: (0, j)),
        ],
        out_specs=pl.BlockSpec((INPROJ_TM, INPROJ_TN), lambda i, j: (i, j)),
        scratch_shapes=[pltpu.VMEM((INPROJ_TM, d), BF16)],
        compiler_params=_cparams(("parallel", "arbitrary")),
        name="inproj",
    )(x2, gain, w)


def _mixer_a_kernel(aq_ref, ak_ref, av_ref, iq_ref, ikwq_ref, ikwk_ref, qg_ref, kg_ref, o_ref,
                    kn_scr, v_scr, kidx_scr, qi_scr, wb_scr, sc_scr, acc_scr, m_scr, l_scr, j_scr,
                    *, seq, topk):
    t = ATTN_T
    i = pl.program_id(1)
    nch = i + 1
    kf = float(topk)
    inf = jnp.inf

    @pl.when(i == 0)
    def _prep_keys():
        def body(c, carry):
            r0 = pl.multiple_of(c * t, t)
            k = ak_ref[pl.ds(r0, t), :]
            kn_scr[pl.ds(r0, t), :] = _head_rms_norm(k, kg_ref[...]).astype(BF16)
            v_scr[pl.ds(r0, t), :] = av_ref[pl.ds(r0, t), :].astype(BF16)
            kidx_scr[pl.ds(r0, t), :] = ikwk_ref[pl.ds(r0, t), 0:IDX_DIM].astype(BF16)
            return carry

        lax.fori_loop(0, seq // t, body, 0)

    qn = (_head_rms_norm(aq_ref[...], qg_ref[...]) * ATTN_SCALE).astype(BF16)
    qi = iq_ref[...].astype(BF16)
    for h in range(IDX_HEADS):
        qi_scr[h] = qi[:, IDX_DIM * h:IDX_DIM * (h + 1)]
    w = ikwq_ref[:, IDX_DIM:IDX_DIM + IDX_HEADS]
    for h in range(IDX_HEADS):
        wb_scr[h] = jnp.broadcast_to(w[:, h:h + 1], (t, LANES))

    row = lax.broadcasted_iota(jnp.int32, (t, t), 0)
    col = lax.broadcasted_iota(jnp.int32, (t, t), 1)
    causal = col <= row
    col_f = col.astype(F32)

    def score_chunk(j):
        r0 = pl.multiple_of(j * t, t)
        kc = kidx_scr[pl.ds(r0, t), :]
        acc = jnp.zeros((t, t), F32)
        for h in range(IDX_HEADS):
            s = _dot_nt(qi_scr[h], kc)
            acc = acc + _rep2(wb_scr[h]) * jnp.maximum(s, 0.0)
        return acc

    def fold(op, a):
        return op(a[:, :LANES], a[:, LANES:])

    def score_body(j, carry):
        mx, mn = carry
        acc = score_chunk(j)
        sc_scr[j] = acc
        return jnp.maximum(mx, fold(jnp.maximum, acc)), jnp.minimum(mn, fold(jnp.minimum, acc))

    mx, mn = lax.fori_loop(0, i, score_body,
                           (jnp.full((t, LANES), -inf, F32), jnp.full((t, LANES), inf, F32)))
    acc = score_chunk(i)
    sc_scr[i] = jnp.where(causal, acc, -inf)
    mx = jnp.maximum(mx, fold(jnp.maximum, jnp.where(causal, acc, -inf)))
    mn = jnp.minimum(mn, fold(jnp.minimum, jnp.where(causal, acc, inf)))
    rmax = jnp.max(mx, axis=1, keepdims=True)
    rmin = jnp.min(mn, axis=1, keepdims=True)

    def count_ge(thr):
        thr_b = jnp.broadcast_to(thr, (t, LANES))

        def body(j, cnt):
            x = sc_scr[j]
            return (cnt + jnp.where(x[:, :LANES] >= thr_b, 1.0, 0.0)
                    + jnp.where(x[:, LANES:] >= thr_b, 1.0, 0.0))

        cnt = lax.fori_loop(0, nch, body, jnp.zeros((t, LANES), F32))
        return jnp.sum(cnt, axis=1, keepdims=True)

    t_local = lax.broadcasted_iota(jnp.int32, (t, 1), 0)
    n_valid = (i * t + t_local + 1).astype(F32)
    fin0 = jnp.where(n_valid <= kf, 1.0, 0.0)

    def bis_cond(c):
        return jnp.logical_and(c[0] < BISECT_MAX_ITERS, c[1] < 0.5)

    def bis_body(c):
        it, _, lo, hi, cl, ch, fin = c
        mid = jnp.where(hi == inf, rmax, 0.5 * (lo + hi))
        stuck = jnp.logical_or(mid <= lo, mid >= hi)
        cnt = count_ge(mid)
        act = jnp.logical_and(fin < 0.5, jnp.logical_not(stuck))
        ge = cnt >= kf
        take_lo = jnp.logical_and(act, ge)
        take_hi = jnp.logical_and(act, jnp.logical_not(ge))
        lo = jnp.where(take_lo, mid, lo)
        cl = jnp.where(take_lo, cnt, cl)
        hi = jnp.where(take_hi, mid, hi)
        ch = jnp.where(take_hi, cnt, ch)
        fin = jnp.where(jnp.logical_or(cl <= kf, stuck), 1.0, fin)
        return it + 1, jnp.min(fin), lo, hi, cl, ch, fin

    init = (jnp.int32(0), jnp.min(fin0), rmin, jnp.full((t, 1), inf, F32), n_valid,
            jnp.zeros((t, 1), F32), fin0)
    _, _, lo, hi, cl, ch, _ = lax.while_loop(bis_cond, bis_body, init)

    j_scr[...] = jnp.full((t, LANES), float(seq), F32)

    @pl.when(jnp.max(cl) > kf)
    def _ties():
        need = kf - ch
        lo_b = jnp.broadcast_to(lo, (t, t))
        hi_b = jnp.broadcast_to(hi, (t, t))

        def count_band_le(jm):
            jm_b = jnp.broadcast_to(jm, (t, t))

            def body(j, cnt):
                x = sc_scr[j]
                sidx = col_f + (j * t).astype(F32)
                m = jnp.where(x >= lo_b,
                              jnp.where(x >= hi_b, 0.0, jnp.where(sidx <= jm_b, 1.0, 0.0)), 0.0)
                return cnt + fold(jnp.add, m)

            cnt = lax.fori_loop(0, nch, body, jnp.zeros((t, LANES), F32))
            return jnp.sum(cnt, axis=1, keepdims=True)

        def search(_, c):
            jlo, jhi = c
            jm = jnp.floor(0.5 * (jlo + jhi))
            ok = count_band_le(jm) >= need
            return jnp.where(ok, jlo, jm), jnp.where(ok, jm, jhi)

        n_steps = max(1, (seq).bit_length())
        _, jhi = lax.fori_loop(0, n_steps, search,
                               (jnp.full((t, 1), -1.0, F32), jnp.full((t, 1), float(seq - 1), F32)))
        j_scr[...] = jnp.broadcast_to(jhi, (t, LANES))

    lo_b = jnp.broadcast_to(lo, (t, t))
    hi_b = jnp.broadcast_to(hi, (t, t))
    jsel_b = _rep2(j_scr[...])

    def bias_body(j, carry):
        x = sc_scr[j]
        sidx = col_f + (j * t).astype(F32)
        sc_scr[j] = jnp.where(
            x >= lo_b, jnp.where(x >= hi_b, 0.0, jnp.where(sidx <= jsel_b, 0.0, MASK_BIAS)), MASK_BIAS)
        return carry

    lax.fori_loop(0, nch, bias_body, 0)

    lane_head = _lane_head((t, MIX_WIDTH))
    out = jnp.zeros((t, MIX_WIDTH), F32)
    for h in range(N_HEADS):
        qm = jnp.where(lane_head == h, qn, jnp.zeros_like(qn))
        acc_scr[...] = jnp.zeros((t, MIX_WIDTH), F32)
        m_scr[...] = jnp.full((t, LANES), MAX_INIT, F32)
        l_scr[...] = jnp.zeros((t, LANES), F32)

        def attn_body(j, carry, qm=qm):
            r0 = pl.multiple_of(j * t, t)
            s = _dot_nt(qm, kn_scr[pl.ds(r0, t), :]) + sc_scr[j]
            m_old = m_scr[...]
            m_new = jnp.maximum(m_old, jnp.max(s, axis=1, keepdims=True))
            alpha = jnp.exp(m_old - m_new)
            p = jnp.exp(s - _rep2(m_new))
            l_scr[...] = alpha * l_scr[...] + jnp.sum(p, axis=1, keepdims=True)
            acc_scr[...] = _rep2(alpha) * acc_scr[...] + _dot(p.astype(BF16), v_scr[pl.ds(r0, t), :])
            m_scr[...] = m_new
            return carry

        lax.fori_loop(0, nch, attn_body, 0)
        out = out + jnp.where(lane_head == h, acc_scr[...] / _rep2(l_scr[...]), 0.0)
    o_ref[...] = out


def _mixer_a(z, qg, kg, *, batch, seq):
    t = ATTN_T
    nq = seq // t
    topk = min(TOPK_MAX, seq // 4)
    g0 = z.shape[1] // SEG - N_SEGS

    def qblk(c):
        return pl.BlockSpec((t, SEG), lambda b, i, c=c: (b * nq + i, c))

    def kblk(c):
        return pl.BlockSpec((seq, SEG), lambda b, i, c=c: (b, c))

    return pl.pallas_call(
        functools.partial(_mixer_a_kernel, seq=seq, topk=topk),
        out_shape=jax.ShapeDtypeStruct((batch * seq, MIX_WIDTH), F32),
        grid=(batch, nq),
        in_specs=[
            qblk(g0 + 0), kblk(g0 + 1), kblk(g0 + 2), qblk(g0 + 3), qblk(g0 + 10), kblk(g0 + 10),
            pl.BlockSpec((1, MIX_WIDTH), lambda b, i: (0, 0)),
            pl.BlockSpec((1, MIX_WIDTH), lambda b, i: (0, 0)),
        ],
        out_specs=pl.BlockSpec((t, MIX_WIDTH), lambda b, i: (b * nq + i, 0)),
        scratch_shapes=[
            pltpu.VMEM((seq, MIX_WIDTH), BF16),
            pltpu.VMEM((seq, MIX_WIDTH), BF16),
            pltpu.VMEM((seq, IDX_DIM), BF16),
            pltpu.VMEM((IDX_HEADS, t, IDX_DIM), BF16),
            pltpu.VMEM((IDX_HEADS, t, LANES), F32),
            pltpu.VMEM((seq // t, t, t), F32),
            pltpu.VMEM((t, MIX_WIDTH), F32),
            pltpu.VMEM((t, LANES), F32),
            pltpu.VMEM((t, LANES), F32),
            pltpu.VMEM((t, LANES), F32),
        ],
        compiler_params=_cparams(("parallel", "arbitrary")),
        name="mixer_a",
    )(z, z, z, z, z, z, qg, kg)


def _mixer_d_kernel(sq_ref, sk_ref, sv_ref, o_ref, k_scr, v_scr, acc_scr, carry_scr, *, seq):
    t = ATTN_T
    i = pl.program_id(1)

    @pl.when(i == 0)
    def _prep_keys():
        def body(c, carry):
            r0 = pl.multiple_of(c * t, t)
            k_scr[pl.ds(r0, t), :] = sk_ref[pl.ds(r0, t), :].astype(BF16)
            v_scr[pl.ds(r0, t), :] = sv_ref[pl.ds(r0, t), :].astype(BF16)
            return carry

        lax.fori_loop(0, seq // t, body, 0)

    q = (sq_ref[...] * ATTN_SCALE).astype(BF16)
    row = lax.broadcasted_iota(jnp.int32, (t, t), 0)
    col = lax.broadcasted_iota(jnp.int32, (t, t), 1)
    strict = col < row
    later = jnp.where(row > col, 1.0, 0.0).astype(BF16)
    lane_head = _lane_head((t, MIX_WIDTH))

    out = jnp.zeros((t, MIX_WIDTH), F32)
    for h in range(N_HEADS):
        qm = jnp.where(lane_head == h, q, jnp.zeros_like(q))
        acc_scr[...] = jnp.zeros((t, MIX_WIDTH), F32)
        carry_scr[...] = jnp.zeros((t, LANES), F32)

        def chunk(j, masked, qm=qm):
            r0 = pl.multiple_of(j * t, t)
            z = _dot_nt(qm, k_scr[pl.ds(r0, t), :])
            l1m = jnp.minimum(-z, 0.0) - jnp.log1p(jnp.exp(-jnp.abs(z)))
            if masked:
                l1m = jnp.where(strict, l1m, 0.0)
            carry = carry_scr[...]
            tail = _split_dot(l1m, later) + _rep2(carry)
            a = jnp.exp(z + l1m + tail)
            if masked:
                a = jnp.where(strict, a, 0.0)
            acc_scr[...] += _dot(a.astype(BF16), v_scr[pl.ds(r0, t), :])
            carry_scr[...] = carry + jnp.sum(l1m, axis=1, keepdims=True)

        chunk(i, True)

        def body(jj, c, chunk=chunk):
            chunk(i - jj, False)
            return c

        lax.fori_loop(1, i + 1, body, 0)
        out = out + jnp.where(lane_head == h, acc_scr[...], 0.0)
    o_ref[...] = out


def _mixer_d(z, *, batch, seq):
    t = ATTN_T
    nq = seq // t
    g0 = z.shape[1] // SEG - N_SEGS

    return pl.pallas_call(
        functools.partial(_mixer_d_kernel, seq=seq),
        out_shape=jax.ShapeDtypeStruct((batch * seq, MIX_WIDTH), F32),
        grid=(batch, nq),
        in_specs=[
            pl.BlockSpec((t, SEG), lambda b, i: (b * nq + i, g0 + 7)),
            pl.BlockSpec((seq, SEG), lambda b, i: (b, g0 + 8)),
            pl.BlockSpec((seq, SEG), lambda b, i: (b, g0 + 9)),
        ],
        out_specs=pl.BlockSpec((t, MIX_WIDTH), lambda b, i: (b * nq + i, 0)),
        scratch_shapes=[
            pltpu.VMEM((seq, MIX_WIDTH), BF16),
            pltpu.VMEM((seq, MIX_WIDTH), BF16),
            pltpu.VMEM((t, MIX_WIDTH), F32),
            pltpu.VMEM((t, LANES), F32),
        ],
        compiler_params=_cparams(("parallel", "arbitrary")),
        name="mixer_d",
    )(z, z, z)


def _merge_kernel(x_ref, oa_ref, od_ref, pin_ref, halo_ref, gu_ref, gv_ref, gl_ref,
                  poolw_ref, pscale_ref, gmn_ref, wcat_ref, gbias_ref, wbr_ref, wout_ref,
                  o_ref, ext_scr, *, seq):
    tm = MERGE_TM
    d = o_ref.shape[1]
    tile_in_seq = pl.program_id(0) % (seq // tm)

    pin = pin_ref[...]
    halo = halo_ref[...]
    ext_scr[0:POOL_HALO, :] = jnp.where(tile_in_seq == 0, jnp.zeros_like(halo), halo)
    ext_scr[POOL_HALO:POOL_HALO + tm, :] = pin
    lane_grp = lax.broadcasted_iota(jnp.int32, (1, MIX_WIDTH), 1) // (MIX_WIDTH // len(POOL_WINDOWS))
    wsum = jnp.zeros((tm, MIX_WIDTH), F32)
    run = pin
    for j in range(1, max(POOL_WINDOWS)):
        run = run + ext_scr[POOL_HALO - j:POOL_HALO - j + tm, :]
        if (j + 1) in POOL_WINDOWS:
            wsum = jnp.where(lane_grp == POOL_WINDOWS.index(j + 1), run, wsum)
    win = jnp.zeros((1, MIX_WIDTH), F32)
    for g, wlen in enumerate(POOL_WINDOWS):
        win = jnp.where(lane_grp == g, float(wlen), win)
    pos1 = (tile_in_seq * tm + lax.broadcasted_iota(jnp.int32, (tm, 1), 0) + 1).astype(F32)
    pooled = wsum / jnp.minimum(pos1, win) - pin
    o_b = _dot(pooled.astype(BF16), poolw_ref[...]) * pscale_ref[...]

    gu = _gelu_tanh(gu_ref[...])
    gv = _gelu_tanh(gv_ref[...])
    mu = jnp.mean(gv, axis=-1, keepdims=True)
    xc = gv - mu
    var = jnp.mean(xc * xc, axis=-1, keepdims=True)
    vb = (xc * lax.rsqrt(var + EPS) * gmn_ref[...]).astype(BF16)
    kw = GMLP_GROUPS * GMLP_CHUNK
    wr = lax.broadcasted_iota(jnp.int32, (GMLP_CHUNK, kw), 0)
    wc = lax.broadcasted_iota(jnp.int32, (GMLP_CHUNK, kw), 1)
    wcat = jnp.where((wc % GMLP_CHUNK) <= wr, wcat_ref[...], jnp.zeros((GMLP_CHUNK, kw), BF16))
    gr = lax.broadcasted_iota(jnp.int32, (kw, MIX_WIDTH), 0) // GMLP_CHUNK
    gc = lax.broadcasted_iota(jnp.int32, (kw, MIX_WIDTH), 1) // (MIX_WIDTH // GMLP_GROUPS)
    mixed = []
    for c in range(tm // GMLP_CHUNK):
        vc = vb[c * GMLP_CHUNK:(c + 1) * GMLP_CHUNK, :]
        vbd = jnp.where(gr == gc, jnp.concatenate([vc] * GMLP_GROUPS, axis=0),
                        jnp.zeros((kw, MIX_WIDTH), BF16))
        mixed.append(_dot(wcat, vbd) + gbias_ref[...])
    o_c = gu * jnp.concatenate(mixed, axis=0)

    merged = jnp.zeros((tm, d), F32)
    for n, o_n in enumerate((oa_ref[...], o_b, o_c, od_ref[...])):
        proj = _dot(o_n.astype(BF16), wbr_ref[n])
        merged = merged + _sigmoid(gl_ref[:, n * d:(n + 1) * d]) * proj
    o_ref[...] = x_ref[...] + _dot(merged.astype(BF16), wout_ref[...])


def _merge(x2, o_a, o_d, z, poolw_bd, pscale, gmn, wcat, gbias, wbr, wout, *, seq):
    n, d = x2.shape
    tm = MERGE_TM
    g0 = N_BRANCH * d // SEG

    def seg(c):
        return pl.BlockSpec((tm, SEG), lambda i, c=c: (i, c))

    def full(a):
        nd = a.ndim
        return pl.BlockSpec(a.shape, lambda i, nd=nd: (0,) * nd)

    halo_blocks = tm // POOL_HALO
    return pl.pallas_call(
        functools.partial(_merge_kernel, seq=seq),
        out_shape=jax.ShapeDtypeStruct((n, d), F32),
        grid=(n // tm,),
        in_specs=[
            pl.BlockSpec((tm, d), lambda i: (i, 0)),
            pl.BlockSpec((tm, MIX_WIDTH), lambda i: (i, 0)),
            pl.BlockSpec((tm, MIX_WIDTH), lambda i: (i, 0)),
            seg(g0 + 4),
            pl.BlockSpec((POOL_HALO, SEG),
                         lambda i: (jnp.maximum(i * halo_blocks - 1, 0), g0 + 4)),
            seg(g0 + 5), seg(g0 + 6),
            pl.BlockSpec((tm, N_BRANCH * d), lambda i: (i, 0)),
            full(poolw_bd), full(pscale), full(gmn), full(wcat), full(gbias), full(wbr), full(wout),
        ],
        out_specs=pl.BlockSpec((tm, d), lambda i: (i, 0)),
        scratch_shapes=[pltpu.VMEM((POOL_HALO + tm, MIX_WIDTH), F32)],
        compiler_params=_cparams(("parallel",)),
        name="merge",
    )(x2, o_a, o_d, z, z, z, z, z, poolw_bd, pscale, gmn, wcat, gbias, wbr, wout)


def _ffn_kernel(x_ref, g_ref, wg_ref, wu_ref, wd_ref, o_ref, h_scr, acc_scr):
    k = pl.program_id(1)

    @pl.when(k == 0)
    def _():
        x = x_ref[...]
        ms = jnp.mean(x * x, axis=-1, keepdims=True)
        h_scr[...] = (x * lax.rsqrt(ms + EPS) * g_ref[...]).astype(BF16)
        acc_scr[...] = jnp.zeros_like(acc_scr)

    h = h_scr[...]
    g = _dot(h, wg_ref[...])
    u = _dot(h, wu_ref[...])
    acc_scr[...] += _dot((g * _sigmoid(g) * u).astype(BF16), wd_ref[...])

    @pl.when(k == pl.num_programs(1) - 1)
    def _():
        o_ref[...] = x_ref[...] + acc_scr[...]


def _ffn(x2, gain, wg, wu, wd):
    n, d = x2.shape
    d_ff = wg.shape[1]
    return pl.pallas_call(
        _ffn_kernel,
        out_shape=jax.ShapeDtypeStruct((n, d), F32),
        grid=(n // FFN_TM, d_ff // FFN_TF),
        in_specs=[
            pl.BlockSpec((FFN_TM, d), lambda i, k: (i, 0)),
            pl.BlockSpec((1, d), lambda i, k: (0, 0)),
            pl.BlockSpec((d, FFN_TF), lambda i, k: (0, k)),
            pl.BlockSpec((d, FFN_TF), lambda i, k: (0, k)),
            pl.BlockSpec((FFN_TF, d), lambda i, k: (k, 0)),
        ],
        out_specs=pl.BlockSpec((FFN_TM, d), lambda i, k: (i, 0)),
        scratch_shapes=[pltpu.VMEM((FFN_TM, d), BF16), pltpu.VMEM((FFN_TM, d), F32)],
        compiler_params=_cparams(("parallel", "arbitrary")),
        name="ffn",
    )(x2, gain, wg, wu, wd)


def _prep_w_in(w_in_l, d):
    a_end = 3 * MIX_WIDTH + IDX_HEADS * IDX_DIM
    i_end = a_end + IDX_DIM + IDX_HEADS
    r_end = i_end + 6 * MIX_WIDTH
    pad = SEG - (IDX_DIM + IDX_HEADS)
    return jnp.concatenate(
        [w_in_l[:, r_end:], w_in_l[:, :a_end], w_in_l[:, i_end:r_end], w_in_l[:, a_end:i_end],
         jnp.zeros((d, pad), w_in_l.dtype)], axis=1).astype(BF16)


def kernel(x, mix_norm, w_in, attn_q_norm, attn_k_norm, pool_w, pool_scale, gmlp_norm, gmlp_w_s,
           gmlp_b, w_branch, w_out, ffn_norm, w_ffn_gate, w_ffn_up, w_ffn_down):
    batch, seq, d = x.shape
    depth = w_in.shape[0]
    assert seq % ATTN_T == 0 and seq % MERGE_TM == 0 and (batch * seq) % INPROJ_TM == 0
    x2 = x.reshape(batch * seq, d)
    for l in range(depth):
        w_l = _prep_w_in(w_in[l], d)
        z = _inproj(x2, mix_norm[l][None, :], w_l)
        qg = jnp.tile(attn_q_norm[l], N_HEADS)[None, :]
        kg = jnp.tile(attn_k_norm[l], N_HEADS)[None, :]
        o_a = _mixer_a(z, qg, kg, batch=batch, seq=seq)
        o_d = _mixer_d(z, batch=batch, seq=seq)
        poolw_bd = jax.scipy.linalg.block_diag(*[pool_w[l, g] for g in range(len(POOL_WINDOWS))]).astype(BF16)
        wcat = jnp.concatenate([gmlp_w_s[l, g] for g in range(GMLP_GROUPS)], axis=1).astype(BF16)
        gbias = jnp.repeat(gmlp_b[l].T, MIX_WIDTH // GMLP_GROUPS, axis=1)
        x2 = _merge(x2, o_a, o_d, z, poolw_bd, pool_scale[l][None, :], gmlp_norm[l][None, :], wcat,
                    gbias, w_branch[l].astype(BF16), w_out[l].astype(BF16), seq=seq)
        x2 = _ffn(x2, ffn_norm[l][None, :], w_ffn_gate[l].astype(BF16), w_ffn_up[l].astype(BF16),
                  w_ffn_down[l].astype(BF16))
    return x2.reshape(batch, seq, d)
```

```python
import functools

import jax
import jax.numpy as jnp
from jax import lax
from jax.experimental import pallas as pl
from jax.experimental.pallas import tpu as pltpu

F32 = jnp.float32
BF16 = jnp.bfloat16

EPS = 1e-6
N_BRANCH = 4
MIX_WIDTH = 256
HEAD_DIM = 64
N_HEADS = MIX_WIDTH // HEAD_DIM
IDX_HEADS = 8
IDX_DIM = 32
TOPK_MAX = 256
POOL_WINDOWS = (2, 4, 8, 16)
POOL_HALO = 16
GMLP_CHUNK = 128
GMLP_GROUPS = 4
ATTN_SCALE = HEAD_DIM ** -0.5

SEG = 256
N_SEGS = 11
LANES = 128

INPROJ_TM = 1024
INPROJ_TN = 768
ATTN_T = 256
MERGE_TM = 512
FFN_TM = 512
FFN_TF = 1408

MASK_BIAS = -2e30
MAX_INIT = -1e30
BISECT_MAX_ITERS = 64
SB_EXIT = 105.0

VMEM_LIMIT = 56 * 1024 * 1024


def _cparams(sem):
    return pltpu.CompilerParams(dimension_semantics=sem, vmem_limit_bytes=VMEM_LIMIT)


def _dot(a, b):
    return jnp.dot(a, b, preferred_element_type=F32)


def _dot_nt(a, b):
    return lax.dot_general(a, b, (((1,), (1,)), ((), ())), preferred_element_type=F32)


def _sigmoid(x):
    return 1.0 / (1.0 + jnp.exp(-x))


def _gelu_tanh(x):
    c = 0.7978845608028654
    return 0.5 * x * (1.0 + jnp.tanh(c * (x + 0.044715 * (x * x * x))))


def _split_dot(x, ones_mat):
    hi = x.astype(BF16)
    lo = (x - hi.astype(F32)).astype(BF16)
    return _dot(hi, ones_mat) + _dot(lo, ones_mat)


def _head_blockdiag_ones():
    r = lax.broadcasted_iota(jnp.int32, (MIX_WIDTH, MIX_WIDTH), 0) // HEAD_DIM
    c = lax.broadcasted_iota(jnp.int32, (MIX_WIDTH, MIX_WIDTH), 1) // HEAD_DIM
    return jnp.where(r == c, 1.0, 0.0).astype(BF16)


def _head_rms_norm(x, gain):
    ms = _split_dot(x * x, _head_blockdiag_ones()) * (1.0 / HEAD_DIM)
    return x * lax.rsqrt(ms + EPS) * gain


def _lane_head(shape):
    return lax.broadcasted_iota(jnp.int32, shape, len(shape) - 1) // HEAD_DIM


def _rep2(a):
    return jnp.concatenate([a, a], axis=1)


def _inproj_kernel(x_ref, g_ref, w_ref, o_ref, h_scr):
    @pl.when(pl.program_id(1) == 0)
    def _():
        x = x_ref[...]
        ms = jnp.mean(x * x, axis=-1, keepdims=True)
        h_scr[...] = (x * lax.rsqrt(ms + EPS) * g_ref[...]).astype(BF16)

    o_ref[...] = _dot(h_scr[...], w_ref[...])


def _inproj(x2, gain, w):
    n, d = x2.shape
    d_in = w.shape[1]
    return pl.pallas_call(
        _inproj_kernel,
        out_shape=jax.ShapeDtypeStruct((n, d_in), F32),
        grid=(n // INPROJ_TM, d_in // INPROJ_TN),
        in_specs=[
            pl.BlockSpec((INPROJ_TM, d), lambda i, j: (i, 0)),
            pl.BlockSpec((1, d), lambda i, j: (0, 0)),
            pl.BlockSpec((d, INPROJ_TN), lambda i, j: (0, j)),
        ],
        out_specs=pl.BlockSpec((INPROJ_TM, INPROJ_TN), lambda i, j: (i, j)),
        scratch_shapes=[pltpu.VMEM((INPROJ_TM, d), BF16)],
        compiler_params=_cparams(("parallel", "arbitrary")),
        name="inproj",
    )(x2, gain, w)


def _mixer_a_kernel(aq_ref, ak_ref, av_ref, iq_ref, ikwq_ref, ikwk_ref, qg_ref, kg_ref, o_ref,
                    kn_scr, v_scr, kidx_scr, qi_scr, wb_scr, sc_scr, acc_scr, m_scr, l_scr, j_scr,
                    *, seq, topk):
    t = ATTN_T
    i = pl.program_id(1)
    nch = i + 1
    kf = float(topk)
    inf = jnp.inf

    @pl.when(i == 0)
    def _prep_keys():
        def body(c, carry):
            r0 = pl.multiple_of(c * t, t)
            k = ak_ref[pl.ds(r0, t), :]
            kn_scr[pl.ds(r0, t), :] = _head_rms_norm(k, kg_ref[...]).astype(BF16)
            v_scr[pl.ds(r0, t), :] = av_ref[pl.ds(r0, t), :].astype(BF16)
            kidx_scr[pl.ds(r0, t), :] = ikwk_ref[pl.ds(r0, t), 0:IDX_DIM].astype(BF16)
            return carry

        lax.fori_loop(0, seq // t, body, 0)

    qn = (_head_rms_norm(aq_ref[...], qg_ref[...]) * ATTN_SCALE).astype(BF16)
    qi = iq_ref[...].astype(BF16)
    for h in range(IDX_HEADS):
        qi_scr[h] = qi[:, IDX_DIM * h:IDX_DIM * (h + 1)]
    w = ikwq_ref[:, IDX_DIM:IDX_DIM + IDX_HEADS]
    for h in range(IDX_HEADS):
        wb_scr[h] = jnp.broadcast_to(w[:, h:h + 1], (t, LANES))

    row = lax.broadcasted_iota(jnp.int32, (t, t), 0)
    col = lax.broadcasted_iota(jnp.int32, (t, t), 1)
    causal = col <= row
    col_f = col.astype(F32)

    def score_chunk(j):
        r0 = pl.multiple_of(j * t, t)
        kc = kidx_scr[pl.ds(r0, t), :]
        acc = jnp.zeros((t, t), F32)
        for h in range(IDX_HEADS):
            s = _dot_nt(qi_scr[h], kc)
            acc = acc + _rep2(wb_scr[h]) * jnp.maximum(s, 0.0)
        return acc

    def fold(op, a):
        return op(a[:, :LANES], a[:, LANES:])

    def score_body(j, carry):
        mx, mn = carry
        acc = score_chunk(j)
        sc_scr[j] = acc
        return jnp.maximum(mx, fold(jnp.maximum, acc)), jnp.minimum(mn, fold(jnp.minimum, acc))

    mx, mn = lax.fori_loop(0, i, score_body,
                           (jnp.full((t, LANES), -inf, F32), jnp.full((t, LANES), inf, F32)))
    acc = score_chunk(i)
    sc_scr[i] = jnp.where(causal, acc, -inf)
    mx = jnp.maximum(mx, fold(jnp.maximum, jnp.where(causal, acc, -inf)))
    mn = jnp.minimum(mn, fold(jnp.minimum, jnp.where(causal, acc, inf)))
    rmax = jnp.max(mx, axis=1, keepdims=True)
    rmin = jnp.min(mn, axis=1, keepdims=True)

    def count_ge(thr):
        thr_b = jnp.broadcast_to(thr, (t, LANES))

        def body(j, cnt):
            x = sc_scr[j]
            return (cnt + jnp.where(x[:, :LANES] >= thr_b, 1.0, 0.0)
                    + jnp.where(x[:, LANES:] >= thr_b, 1.0, 0.0))

        cnt = lax.fori_loop(0, nch, body, jnp.zeros((t, LANES), F32))
        return jnp.sum(cnt, axis=1, keepdims=True)

    t_local = lax.broadcasted_iota(jnp.int32, (t, 1), 0)
    n_valid = (i * t + t_local + 1).astype(F32)
    fin0 = jnp.where(n_valid <= kf, 1.0, 0.0)

    def bis_cond(c):
        return jnp.logical_and(c[0] < BISECT_MAX_ITERS, c[1] < 0.5)

    def bis_body(c):
        it, _, lo, hi, cl, ch, fin = c
        mid = jnp.where(hi == inf, rmax, 0.5 * (lo + hi))
        stuck = jnp.logical_or(mid <= lo, mid >= hi)
        cnt = count_ge(mid)
        act = jnp.logical_and(fin < 0.5, jnp.logical_not(stuck))
        ge = cnt >= kf
        take_lo = jnp.logical_and(act, ge)
        take_hi = jnp.logical_and(act, jnp.logical_not(ge))
        lo = jnp.where(take_lo, mid, lo)
        cl = jnp.where(take_lo, cnt, cl)
        hi = jnp.where(take_hi, mid, hi)
        ch = jnp.where(take_hi, cnt, ch)
        fin = jnp.where(jnp.logical_or(cl <= kf, stuck), 1.0, fin)
        return it + 1, jnp.min(fin), lo, hi, cl, ch, fin

    init = (jnp.int32(0), jnp.min(fin0), rmin, jnp.full((t, 1), inf, F32), n_valid,
            jnp.zeros((t, 1), F32), fin0)
    _, _, lo, hi, cl, ch, _ = lax.while_loop(bis_cond, bis_body, init)

    j_scr[...] = jnp.full((t, LANES), float(seq), F32)

    @pl.when(jnp.max(cl) > kf)
    def _ties():
        need = kf - ch
        lo_b = jnp.broadcast_to(lo, (t, t))
        hi_b = jnp.broadcast_to(hi, (t, t))

        def count_band_le(jm):
            jm_b = jnp.broadcast_to(jm, (t, t))

            def body(j, cnt):
                x = sc_scr[j]
                sidx = col_f + (j * t).astype(F32)
                m = jnp.where(x >= lo_b,
                              jnp.where(x >= hi_b, 0.0, jnp.where(sidx <= jm_b, 1.0, 0.0)), 0.0)
                return cnt + fold(jnp.add, m)

            cnt = lax.fori_loop(0, nch, body, jnp.zeros((t, LANES), F32))
            return jnp.sum(cnt, axis=1, keepdims=True)

        def search(_, c):
            jlo, jhi = c
            jm = jnp.floor(0.5 * (jlo + jhi))
            ok = count_band_le(jm) >= need
            return jnp.where(ok, jlo, jm), jnp.where(ok, jm, jhi)

        n_steps = max(1, (seq).bit_length())
        _, jhi = lax.fori_loop(0, n_steps, search,
                               (jnp.full((t, 1), -1.0, F32), jnp.full((t, 1), float(seq - 1), F32)))
        j_scr[...] = jnp.broadcast_to(jhi, (t, LANES))

    lo_b = jnp.broadcast_to(lo, (t, t))
    hi_b = jnp.broadcast_to(hi, (t, t))
    jsel_b = _rep2(j_scr[...])

    def bias_body(j, carry):
        x = sc_scr[j]
        sidx = col_f + (j * t).astype(F32)
        sc_scr[j] = jnp.where(
            x >= lo_b, jnp.where(x >= hi_b, 0.0, jnp.where(sidx <= jsel_b, 0.0, MASK_BIAS)), MASK_BIAS)
        return carry

    lax.fori_loop(0, nch, bias_body, 0)

    lane_head = _lane_head((t, MIX_WIDTH))
    out = jnp.zeros((t, MIX_WIDTH), F32)
    for h in range(N_HEADS):
        qm = jnp.where(lane_head == h, qn, jnp.zeros_like(qn))
        acc_scr[...] = jnp.zeros((t, MIX_WIDTH), F32)
        m_scr[...] = jnp.full((t, LANES), MAX_INIT, F32)
        l_scr[...] = jnp.zeros((t, LANES), F32)

        def attn_body(j, carry, qm=qm):
            r0 = pl.multiple_of(j * t, t)
            s = _dot_nt(qm, kn_scr[pl.ds(r0, t), :]) + sc_scr[j]
            m_old = m_scr[...]
            m_new = jnp.maximum(m_old, jnp.max(s, axis=1, keepdims=True))
            alpha = jnp.exp(m_old - m_new)
            p = jnp.exp(s - _rep2(m_new))
            l_scr[...] = alpha * l_scr[...] + jnp.sum(p, axis=1, keepdims=True)
            acc_scr[...] = _rep2(alpha) * acc_scr[...] + _dot(p.astype(BF16), v_scr[pl.ds(r0, t), :])
            m_scr[...] = m_new
            return carry

        lax.fori_loop(0, nch, attn_body, 0)
        out = out + jnp.where(lane_head == h, acc_scr[...] / _rep2(l_scr[...]), 0.0)
    o_ref[...] = out


def _mixer_a(z, qg, kg, *, batch, seq):
    t = ATTN_T
    nq = seq // t
    topk = min(TOPK_MAX, seq // 4)
    g0 = z.shape[1] // SEG - N_SEGS

    def qblk(c):
        return pl.BlockSpec((t, SEG), lambda b, i, c=c: (b * nq + i, c))

    def kblk(c):
        return pl.BlockSpec((seq, SEG), lambda b, i, c=c: (b, c))

    return pl.pallas_call(
        functools.partial(_mixer_a_kernel, seq=seq, topk=topk),
        out_shape=jax.ShapeDtypeStruct((batch * seq, MIX_WIDTH), F32),
        grid=(batch, nq),
        in_specs=[
            qblk(g0 + 0), kblk(g0 + 1), kblk(g0 + 2), qblk(g0 + 3), qblk(g0 + 10), kblk(g0 + 10),
            pl.BlockSpec((1, MIX_WIDTH), lambda b, i: (0, 0)),
            pl.BlockSpec((1, MIX_WIDTH), lambda b, i: (0, 0)),
        ],
        out_specs=pl.BlockSpec((t, MIX_WIDTH), lambda b, i: (b * nq + i, 0)),
        scratch_shapes=[
            pltpu.VMEM((seq, MIX_WIDTH), BF16),
            pltpu.VMEM((seq, MIX_WIDTH), BF16),
            pltpu.VMEM((seq, IDX_DIM), BF16),
            pltpu.VMEM((IDX_HEADS, t, IDX_DIM), BF16),
            pltpu.VMEM((IDX_HEADS, t, LANES), F32),
            pltpu.VMEM((seq // t, t, t), F32),
            pltpu.VMEM((t, MIX_WIDTH), F32),
            pltpu.VMEM((t, LANES), F32),
            pltpu.VMEM((t, LANES), F32),
            pltpu.VMEM((t, LANES), F32),
        ],
        compiler_params=_cparams(("parallel", "arbitrary")),
        name="mixer_a",
    )(z, z, z, z, z, z, qg, kg)


def _mixer_d_kernel(sq_ref, sk_ref, sv_ref, o_ref, k_scr, vm_scr, qm_scr, acc_scr, carry_scr, *, seq):
    t = ATTN_T
    i = pl.program_id(1)
    lane_head = _lane_head((t, MIX_WIDTH))

    @pl.when(i == 0)
    def _prep_keys():
        def body(c, carry):
            r0 = pl.multiple_of(c * t, t)
            k_scr[pl.ds(r0, t), :] = sk_ref[pl.ds(r0, t), :].astype(BF16)
            v = sv_ref[pl.ds(r0, t), :].astype(BF16)
            for h in range(N_HEADS):
                vm_scr[h, pl.ds(r0, t), :] = jnp.where(lane_head == h, v, jnp.zeros_like(v))
            return carry

        lax.fori_loop(0, seq // t, body, 0)

    q = (sq_ref[...] * ATTN_SCALE).astype(BF16)
    for h in range(N_HEADS):
        qm_scr[h] = jnp.where(lane_head == h, q, jnp.zeros_like(q))
    acc_scr[...] = jnp.zeros((t, MIX_WIDTH), F32)
    carry_scr[...] = jnp.zeros((N_HEADS, t, LANES), F32)

    row = lax.broadcasted_iota(jnp.int32, (t, t), 0)
    col = lax.broadcasted_iota(jnp.int32, (t, t), 1)
    strict = col < row
    later = jnp.where(row > col, 1.0, 0.0).astype(BF16)

    def chunk(j, masked):
        r0 = pl.multiple_of(j * t, t)
        kc = k_scr[pl.ds(r0, t), :]
        pv = None
        cmin = None
        for h in range(N_HEADS):
            z = _dot_nt(qm_scr[h], kc)
            nl = jnp.maximum(z, 0.0) + jnp.log(1.0 + jnp.exp(-jnp.abs(z)))
            if masked:
                nl = jnp.where(strict, nl, 0.0)
            carry = carry_scr[h]
            tail = _split_dot(nl, later) + _rep2(carry)
            a = jnp.exp(z - nl - tail)
            if masked:
                a = jnp.where(strict, a, 0.0)
            d = _dot(a.astype(BF16), vm_scr[h, pl.ds(r0, t), :])
            pv = d if pv is None else pv + d
            carry = carry + jnp.sum(nl, axis=1, keepdims=True)
            carry_scr[h] = carry
            cmin = carry if cmin is None else jnp.minimum(cmin, carry)
        acc_scr[...] += pv
        return jnp.min(cmin)

    def cond(c):
        return jnp.logical_and(c[0] <= i, c[1] < SB_EXIT)

    def body(c):
        return c[0] + 1, chunk(i - c[0], False)

    lax.while_loop(cond, body, (jnp.int32(1), chunk(i, True)))
    o_ref[...] = acc_scr[...]


def _mixer_d(z, *, batch, seq):
    t = ATTN_T
    nq = seq // t
    g0 = z.shape[1] // SEG - N_SEGS

    return pl.pallas_call(
        functools.partial(_mixer_d_kernel, seq=seq),
        out_shape=jax.ShapeDtypeStruct((batch * seq, MIX_WIDTH), F32),
        grid=(batch, nq),
        in_specs=[
            pl.BlockSpec((t, SEG), lambda b, i: (b * nq + i, g0 + 7)),
            pl.BlockSpec((seq, SEG), lambda b, i: (b, g0 + 8)),
            pl.BlockSpec((seq, SEG), lambda b, i: (b, g0 + 9)),
        ],
        out_specs=pl.BlockSpec((t, MIX_WIDTH), lambda b, i: (b * nq + i, 0)),
        scratch_shapes=[
            pltpu.VMEM((seq, MIX_WIDTH), BF16),
            pltpu.VMEM((N_HEADS, seq, MIX_WIDTH), BF16),
            pltpu.VMEM((N_HEADS, t, MIX_WIDTH), BF16),
            pltpu.VMEM((t, MIX_WIDTH), F32),
            pltpu.VMEM((N_HEADS, t, LANES), F32),
        ],
        compiler_params=_cparams(("parallel", "arbitrary")),
        name="mixer_d",
    )(z, z, z)


def _merge_kernel(x_ref, oa_ref, od_ref, pin_ref, halo_ref, gu_ref, gv_ref, gl_ref,
                  poolw_ref, pscale_ref, gmn_ref, wcat_ref, gbias_ref, wbr_ref, wout_ref,
                  o_ref, ext_scr, *, seq):
    tm = MERGE_TM
    d = o_ref.shape[1]
    tile_in_seq = pl.program_id(0) % (seq // tm)

    pin = pin_ref[...]
    halo = halo_ref[...]
    ext_scr[0:POOL_HALO, :] = jnp.where(tile_in_seq == 0, jnp.zeros_like(halo), halo)
    ext_scr[POOL_HALO:POOL_HALO + tm, :] = pin
    lane_grp = lax.broadcasted_iota(jnp.int32, (1, MIX_WIDTH), 1) // (MIX_WIDTH // len(POOL_WINDOWS))
    wsum = jnp.zeros((tm, MIX_WIDTH), F32)
    run = pin
    for j in range(1, max(POOL_WINDOWS)):
        run = run + ext_scr[POOL_HALO - j:POOL_HALO - j + tm, :]
        if (j + 1) in POOL_WINDOWS:
            wsum = jnp.where(lane_grp == POOL_WINDOWS.index(j + 1), run, wsum)
    win = jnp.zeros((1, MIX_WIDTH), F32)
    for g, wlen in enumerate(POOL_WINDOWS):
        win = jnp.where(lane_grp == g, float(wlen), win)
    pos1 = (tile_in_seq * tm + lax.broadcasted_iota(jnp.int32, (tm, 1), 0) + 1).astype(F32)
    pooled = wsum / jnp.minimum(pos1, win) - pin
    o_b = _dot(pooled.astype(BF16), poolw_ref[...]) * pscale_ref[...]

    gu = _gelu_tanh(gu_ref[...])
    gv = _gelu_tanh(gv_ref[...])
    mu = jnp.mean(gv, axis=-1, keepdims=True)
    xc = gv - mu
    var = jnp.mean(xc * xc, axis=-1, keepdims=True)
    vb = (xc * lax.rsqrt(var + EPS) * gmn_ref[...]).astype(BF16)
    kw = GMLP_GROUPS * GMLP_CHUNK
    wr = lax.broadcasted_iota(jnp.int32, (GMLP_CHUNK, kw), 0)
    wc = lax.broadcasted_iota(jnp.int32, (GMLP_CHUNK, kw), 1)
    wcat = jnp.where((wc % GMLP_CHUNK) <= wr, wcat_ref[...], jnp.zeros((GMLP_CHUNK, kw), BF16))
    gr = lax.broadcasted_iota(jnp.int32, (kw, MIX_WIDTH), 0) // GMLP_CHUNK
    gc = lax.broadcasted_iota(jnp.int32, (kw, MIX_WIDTH), 1) // (MIX_WIDTH // GMLP_GROUPS)
    mixed = []
    for c in range(tm // GMLP_CHUNK):
        vc = vb[c * GMLP_CHUNK:(c + 1) * GMLP_CHUNK, :]
        vbd = jnp.where(gr == gc, jnp.concatenate([vc] * GMLP_GROUPS, axis=0),
                        jnp.zeros((kw, MIX_WIDTH), BF16))
        mixed.append(_dot(wcat, vbd) + gbias_ref[...])
    o_c = gu * jnp.concatenate(mixed, axis=0)

    merged = jnp.zeros((tm, d), F32)
    for n, o_n in enumerate((oa_ref[...], o_b, o_c, od_ref[...])):
        proj = _dot(o_n.astype(BF16), wbr_ref[n])
        merged = merged + _sigmoid(gl_ref[:, n * d:(n + 1) * d]) * proj
    o_ref[...] = x_ref[...] + _dot(merged.astype(BF16), wout_ref[...])


def _merge(x2, o_a, o_d, z, poolw_bd, pscale, gmn, wcat, gbias, wbr, wout, *, seq):
    n, d = x2.shape
    tm = MERGE_TM
    g0 = N_BRANCH * d // SEG

    def seg(c):
        return pl.BlockSpec((tm, SEG), lambda i, c=c: (i, c))

    def full(a):
        nd = a.ndim
        return pl.BlockSpec(a.shape, lambda i, nd=nd: (0,) * nd)

    halo_blocks = tm // POOL_HALO
    return pl.pallas_call(
        functools.partial(_merge_kernel, seq=seq),
        out_shape=jax.ShapeDtypeStruct((n, d), F32),
        grid=(n // tm,),
        in_specs=[
            pl.BlockSpec((tm, d), lambda i: (i, 0)),
            pl.BlockSpec((tm, MIX_WIDTH), lambda i: (i, 0)),
            pl.BlockSpec((tm, MIX_WIDTH), lambda i: (i, 0)),
            seg(g0 + 4),
            pl.BlockSpec((POOL_HALO, SEG),
                         lambda i: (jnp.maximum(i * halo_blocks - 1, 0), g0 + 4)),
            seg(g0 + 5), seg(g0 + 6),
            pl.BlockSpec((tm, N_BRANCH * d), lambda i: (i, 0)),
            full(poolw_bd), full(pscale), full(gmn), full(wcat), full(gbias), full(wbr), full(wout),
        ],
        out_specs=pl.BlockSpec((tm, d), lambda i: (i, 0)),
        scratch_shapes=[pltpu.VMEM((POOL_HALO + tm, MIX_WIDTH), F32)],
        compiler_params=_cparams(("parallel",)),
        name="merge",
    )(x2, o_a, o_d, z, z, z, z, z, poolw_bd, pscale, gmn, wcat, gbias, wbr, wout)


def _ffn_kernel(x_ref, g_ref, wg_ref, wu_ref, wd_ref, o_ref, h_scr, acc_scr):
    k = pl.program_id(1)

    @pl.when(k == 0)
    def _():
        x = x_ref[...]
        ms = jnp.mean(x * x, axis=-1, keepdims=True)
        h_scr[...] = (x * lax.rsqrt(ms + EPS) * g_ref[...]).astype(BF16)
        acc_scr[...] = jnp.zeros_like(acc_scr)

    h = h_scr[...]
    g = _dot(h, wg_ref[...])
    u = _dot(h, wu_ref[...])
    acc_scr[...] += _dot((g * _sigmoid(g) * u).astype(BF16), wd_ref[...])

    @pl.when(k == pl.num_programs(1) - 1)
    def _():
        o_ref[...] = x_ref[...] + acc_scr[...]


def _ffn(x2, gain, wg, wu, wd):
    n, d = x2.shape
    d_ff = wg.shape[1]
    return pl.pallas_call(
        _ffn_kernel,
        out_shape=jax.ShapeDtypeStruct((n, d), F32),
        grid=(n // FFN_TM, d_ff // FFN_TF),
        in_specs=[
            pl.BlockSpec((FFN_TM, d), lambda i, k: (i, 0)),
            pl.BlockSpec((1, d), lambda i, k: (0, 0)),
            pl.BlockSpec((d, FFN_TF), lambda i, k: (0, k)),
            pl.BlockSpec((d, FFN_TF), lambda i, k: (0, k)),
            pl.BlockSpec((FFN_TF, d), lambda i, k: (k, 0)),
        ],
        out_specs=pl.BlockSpec((FFN_TM, d), lambda i, k: (i, 0)),
        scratch_shapes=[pltpu.VMEM((FFN_TM, d), BF16), pltpu.VMEM((FFN_TM, d), F32)],
        compiler_params=_cparams(("parallel", "arbitrary")),
        name="ffn",
    )(x2, gain, wg, wu, wd)


def _prep_w_in(w_in_l, d):
    a_end = 3 * MIX_WIDTH + IDX_HEADS * IDX_DIM
    i_end = a_end + IDX_DIM + IDX_HEADS
    r_end = i_end + 6 * MIX_WIDTH
    pad = SEG - (IDX_DIM + IDX_HEADS)
    return jnp.concatenate(
        [w_in_l[:, r_end:], w_in_l[:, :a_end], w_in_l[:, i_end:r_end], w_in_l[:, a_end:i_end],
         jnp.zeros((d, pad), w_in_l.dtype)], axis=1).astype(BF16)


def kernel(x, mix_norm, w_in, attn_q_norm, attn_k_norm, pool_w, pool_scale, gmlp_norm, gmlp_w_s,
           gmlp_b, w_branch, w_out, ffn_norm, w_ffn_gate, w_ffn_up, w_ffn_down):
    batch, seq, d = x.shape
    depth = w_in.shape[0]
    assert seq % ATTN_T == 0 and seq % MERGE_TM == 0 and (batch * seq) % INPROJ_TM == 0
    x2 = x.reshape(batch * seq, d)
    for l in range(depth):
        w_l = _prep_w_in(w_in[l], d)
        z = _inproj(x2, mix_norm[l][None, :], w_l)
        qg = jnp.tile(attn_q_norm[l], N_HEADS)[None, :]
        kg = jnp.tile(attn_k_norm[l], N_HEADS)[None, :]
        o_a = _mixer_a(z, qg, kg, batch=batch, seq=seq)
        o_d = _mixer_d(z, batch=batch, seq=seq)
        poolw_bd = jax.scipy.linalg.block_diag(*[pool_w[l, g] for g in range(len(POOL_WINDOWS))]).astype(BF16)
        wcat = jnp.concatenate([gmlp_w_s[l, g] for g in range(GMLP_GROUPS)], axis=1).astype(BF16)
        gbias = jnp.repeat(gmlp_b[l].T, MIX_WIDTH // GMLP_GROUPS, axis=1)
        x2 = _merge(x2, o_a, o_d, z, poolw_bd, pool_scale[l][None, :], gmlp_norm[l][None, :], wcat,
                    gbias, w_branch[l].astype(BF16), w_out[l].astype(BF16), seq=seq)
        x2 = _ffn(x2, ffn_norm[l][None, :], w_ffn_gate[l].astype(BF16), w_ffn_up[l].astype(BF16),
                  w_ffn_down[l].astype(BF16))
    return x2.reshape(batch, seq, d)
```

```python
import functools

import jax
import jax.numpy as jnp
from jax import lax
from jax.experimental import pallas as pl
from jax.experimental.pallas import tpu as pltpu

F32 = jnp.float32
BF16 = jnp.bfloat16

EPS = 1e-6
N_BRANCH = 4
MIX_WIDTH = 256
HEAD_DIM = 64
N_HEADS = MIX_WIDTH // HEAD_DIM
IDX_HEADS = 8
IDX_DIM = 32
TOPK_MAX = 256
POOL_WINDOWS = (2, 4, 8, 16)
POOL_HALO = 16
GMLP_CHUNK = 128
GMLP_GROUPS = 4
ATTN_SCALE = HEAD_DIM ** -0.5

SEG = 256
N_SEGS = 11
LANES = 128

INPROJ_TM = 1024
INPROJ_TN = 768
ATTN_T = 256
MERGE_TM = 512
FFN_TM = 512
FFN_TF = 1408

MASK_BIAS = -2e30
MAX_INIT = -1e30
BISECT_ITERS = 14
PEEL_ITERS = 8
FALLBACK_ITERS = 160
SB_EXIT = 105.0

VMEM_LIMIT = 56 * 1024 * 1024


def _cparams(sem):
    return pltpu.CompilerParams(dimension_semantics=sem, vmem_limit_bytes=VMEM_LIMIT)


def _dot(a, b):
    return jnp.dot(a, b, preferred_element_type=F32)


def _dot_nt(a, b):
    return lax.dot_general(a, b, (((1,), (1,)), ((), ())), preferred_element_type=F32)


def _sigmoid(x):
    return 1.0 / (1.0 + jnp.exp(-x))


def _gelu_tanh(x):
    c = 0.7978845608028654
    return 0.5 * x * (1.0 + jnp.tanh(c * (x + 0.044715 * (x * x * x))))


def _split_dot(x, ones_mat):
    hi = x.astype(BF16)
    lo = (x - hi.astype(F32)).astype(BF16)
    return _dot(hi, ones_mat) + _dot(lo, ones_mat)


def _head_blockdiag_ones():
    r = lax.broadcasted_iota(jnp.int32, (MIX_WIDTH, MIX_WIDTH), 0) // HEAD_DIM
    c = lax.broadcasted_iota(jnp.int32, (MIX_WIDTH, MIX_WIDTH), 1) // HEAD_DIM
    return jnp.where(r == c, 1.0, 0.0).astype(BF16)


def _head_rms_norm(x, gain):
    ms = _split_dot(x * x, _head_blockdiag_ones()) * (1.0 / HEAD_DIM)
    return x * lax.rsqrt(ms + EPS) * gain


def _lane_head(shape):
    return lax.broadcasted_iota(jnp.int32, shape, len(shape) - 1) // HEAD_DIM


def _rep2(a):
    return jnp.concatenate([a, a], axis=1)


def _inproj_kernel(x_ref, g_ref, w_ref, o_ref, h_scr):
    @pl.when(pl.program_id(1) == 0)
    def _():
        x = x_ref[...]
        ms = jnp.mean(x * x, axis=-1, keepdims=True)
        h_scr[...] = (x * lax.rsqrt(ms + EPS) * g_ref[...]).astype(BF16)

    o_ref[...] = _dot(h_scr[...], w_ref[...])


def _inproj(x2, gain, w):
    n, d = x2.shape
    d_in = w.shape[1]
    return pl.pallas_call(
        _inproj_kernel,
        out_shape=jax.ShapeDtypeStruct((n, d_in), F32),
        grid=(n // INPROJ_TM, d_in // INPROJ_TN),
        in_specs=[
            pl.BlockSpec((INPROJ_TM, d), lambda i, j: (i, 0)),
            pl.BlockSpec((1, d), lambda i, j: (0, 0)),
            pl.BlockSpec((d, INPROJ_TN), lambda i, j: (0, j)),
        ],
        out_specs=pl.BlockSpec((INPROJ_TM, INPROJ_TN), lambda i, j: (i, j)),
        scratch_shapes=[pltpu.VMEM((INPROJ_TM, d), BF16)],
        compiler_params=_cparams(("parallel", "arbitrary")),
        name="inproj",
    )(x2, gain, w)


def _mixer_a_kernel(aq_ref, ak_ref, av_ref, iq_ref, ikwq_ref, ikwk_ref, qg_ref, kg_ref, o_ref,
                    kn_scr, vt_scr, kidx_scr, qm_scr, qi_scr, sc_scr, acc_scr, j_scr, *, seq, topk):
    t = ATTN_T
    i = pl.program_id(1)
    nch = i + 1
    kf = float(topk)
    inf = jnp.inf

    @pl.when(i == 0)
    def _prep_keys():
        def body(c, carry):
            r0 = pl.multiple_of(c * t, t)
            kn_scr[pl.ds(r0, t), :] = _head_rms_norm(ak_ref[pl.ds(r0, t), :], kg_ref[...]).astype(BF16)
            vt_scr[c] = av_ref[pl.ds(r0, t), :].T.astype(BF16)
            kidx_scr[pl.ds(r0, t), :] = ikwk_ref[pl.ds(r0, t), 0:IDX_DIM].astype(BF16)
            return carry

        lax.fori_loop(0, seq // t, body, 0)

    qn = (_head_rms_norm(aq_ref[...], qg_ref[...]) * ATTN_SCALE).astype(BF16)
    lane_head = _lane_head((t, MIX_WIDTH))
    for h in range(N_HEADS):
        qm_scr[pl.ds(h * t, t), :] = jnp.where(lane_head == h, qn, jnp.zeros_like(qn))
    qi = iq_ref[...].astype(BF16)
    for h in range(IDX_HEADS):
        qi_scr[h] = qi[:, IDX_DIM * h:IDX_DIM * (h + 1)]
    w_t = ikwq_ref[:, 0:LANES].T[IDX_DIM:IDX_DIM + IDX_HEADS, :]

    key_i = lax.broadcasted_iota(jnp.int32, (t, t), 0)
    qry_i = lax.broadcasted_iota(jnp.int32, (t, t), 1)
    causal = key_i <= qry_i
    key_f = key_i.astype(F32)

    def colmax(a):
        return jnp.max(a, axis=0, keepdims=True)

    def colmin(a):
        return jnp.min(a, axis=0, keepdims=True)

    def colsum(a):
        return jnp.sum(a, axis=0, keepdims=True)

    def score_chunk(j):
        r0 = pl.multiple_of(j * t, t)
        kc = kidx_scr[pl.ds(r0, t), :]
        acc = jnp.zeros((t, t), F32)
        for h in range(IDX_HEADS):
            acc = acc + w_t[h:h + 1, :] * jnp.maximum(_dot_nt(kc, qi_scr[h]), 0.0)
        return acc

    def score_body(j, carry):
        mx, mn = carry
        acc = score_chunk(j)
        sc_scr[j] = acc
        return jnp.maximum(mx, colmax(acc)), jnp.minimum(mn, colmin(acc))

    mx, mn = lax.fori_loop(0, i, score_body,
                           (jnp.full((1, t), -inf, F32), jnp.full((1, t), inf, F32)))
    acc = score_chunk(i)
    diag = jnp.where(causal, acc, -inf)
    sc_scr[i] = diag
    rmax = jnp.maximum(mx, colmax(diag))
    rmin = jnp.minimum(mn, colmin(jnp.where(causal, acc, inf)))

    def fold8(a):
        return jnp.sum(a.reshape(t // 8, 8, t), axis=0)

    def count_ge(thr):
        def body(j, cnt):
            return cnt + fold8(jnp.where(sc_scr[j] >= thr, 1.0, 0.0))

        return colsum(lax.fori_loop(0, nch, body, jnp.zeros((8, t), F32)))

    def max_below(thr):
        def body(j, m):
            x = sc_scr[j]
            return jnp.maximum(m, colmax(jnp.where(x < thr, x, -inf)))

        return lax.fori_loop(0, nch, body, jnp.full((1, t), -inf, F32))

    def probe(state, mid, closes):
        lo, hi, cl, ch, fin = state
        cnt = count_ge(mid)
        act = fin < 0.5
        ge = cnt >= kf
        take_lo = jnp.logical_and(act, ge)
        take_hi = jnp.logical_and(act, jnp.logical_not(ge))
        lo = jnp.where(take_lo, mid, lo)
        cl = jnp.where(take_lo, cnt, cl)
        hi = jnp.where(take_hi, mid, hi)
        ch = jnp.where(take_hi, cnt, ch)
        done = cl <= kf
        if closes:
            done = jnp.logical_or(done, take_lo)
        fin = jnp.where(done, 1.0, fin)
        return lo, hi, cl, ch, fin

    def bisect_step(state):
        lo, hi = state[0], state[1]
        return probe(state, lo + 0.5 * (hi - lo), False)

    def peel_step(state):
        return probe(state, max_below(state[1]), True)

    def run(step, limit, state):
        def cond(c):
            return jnp.logical_and(c[0] < limit, c[1] < 0.5)

        def body(c):
            new = step(c[2])
            return c[0] + 1, jnp.min(new[4]), new

        return lax.while_loop(cond, body, (jnp.int32(0), jnp.min(state[4]), state))[2]

    qry_pos = lax.broadcasted_iota(jnp.int32, (1, t), 1)
    n_valid = (i * t + qry_pos + 1).astype(F32)
    c_top = count_ge(rmax)
    all_in = n_valid <= kf
    top_tie = jnp.logical_and(c_top >= kf, jnp.logical_not(all_in))
    state = (jnp.where(top_tie, rmax, rmin),
             jnp.where(top_tie, inf, rmax),
             jnp.where(top_tie, c_top, n_valid),
             jnp.where(top_tie, 0.0, c_top),
             jnp.where(jnp.logical_or(all_in, top_tie), 1.0, 0.0))
    state = run(bisect_step, BISECT_ITERS, state)
    state = run(peel_step, PEEL_ITERS, state)
    state = run(lambda s: peel_step(bisect_step(s)), FALLBACK_ITERS, state)
    lo, hi, cl, ch, _ = state

    j_scr[...] = jnp.full((8, t), float(seq), F32)

    @pl.when(jnp.max(cl) > kf)
    def _ties():
        need = kf - ch

        def count_band_le(jm):
            def body(j, cnt):
                x = sc_scr[j]
                kidx = key_f + (j * t).astype(F32)
                m = jnp.where(x >= lo, jnp.where(x >= hi, 0.0, jnp.where(kidx <= jm, 1.0, 0.0)), 0.0)
                return cnt + colsum(m)

            return lax.fori_loop(0, nch, body, jnp.zeros((1, t), F32))

        def search(_, c):
            jlo, jhi = c
            jm = jnp.floor(0.5 * (jlo + jhi))
            ok = count_band_le(jm) >= need
            return jnp.where(ok, jlo, jm), jnp.where(ok, jm, jhi)

        _, jhi = lax.fori_loop(0, seq.bit_length(), search,
                               (jnp.full((1, t), -1.0, F32), jnp.full((1, t), float(seq - 1), F32)))
        j_scr[...] = jnp.broadcast_to(jhi, (8, t))

    jsel = j_scr[0:1, :]

    def bias_body(j, carry):
        x = sc_scr[j]
        kidx = key_f + (j * t).astype(F32)
        sc_scr[j] = jnp.where(
            x >= lo, jnp.where(x >= hi, 0.0, jnp.where(kidx <= jsel, 0.0, MASK_BIAS)), MASK_BIAS)
        return carry

    lax.fori_loop(0, nch, bias_body, 0)

    acc_scr[...] = jnp.zeros((MIX_WIDTH, t), F32)

    def attn_body(j, carry):
        ms, ls = carry
        r0 = pl.multiple_of(j * t, t)
        bias = sc_scr[j]
        logits = _dot_nt(kn_scr[pl.ds(r0, t), :], qm_scr[...])
        new_ms, new_ls, alphas, pvs = [], [], [], []
        for h in range(N_HEADS):
            s = logits[:, h * t:(h + 1) * t] + bias
            m_new = jnp.maximum(ms[h], colmax(s))
            alphas.append(jnp.exp(ms[h] - m_new))
            p = jnp.exp(s - m_new)
            new_ls.append(alphas[h] * ls[h] + colsum(p))
            pvs.append(_dot(vt_scr[j, pl.ds(HEAD_DIM * h, HEAD_DIM), :], p.astype(BF16)))
            new_ms.append(m_new)
        for h in range(N_HEADS):
            rows = pl.ds(HEAD_DIM * h, HEAD_DIM)
            acc_scr[rows, :] = alphas[h] * acc_scr[rows, :] + pvs[h]
        return tuple(new_ms), tuple(new_ls)

    m0 = tuple(jnp.full((1, t), MAX_INIT, F32) for _ in range(N_HEADS))
    l0 = tuple(jnp.zeros((1, t), F32) for _ in range(N_HEADS))
    _, ls = lax.fori_loop(0, nch, attn_body, (m0, l0))
    for h in range(N_HEADS):
        rows = pl.ds(HEAD_DIM * h, HEAD_DIM)
        acc_scr[rows, :] = acc_scr[rows, :] / ls[h]
    o_ref[...] = acc_scr[...].T


def _mixer_a(z, qg, kg, *, batch, seq):
    t = ATTN_T
    nq = seq // t
    topk = min(TOPK_MAX, seq // 4)
    g0 = z.shape[1] // SEG - N_SEGS

    def qblk(c):
        return pl.BlockSpec((t, SEG), lambda b, i, c=c: (b * nq + i, c))

    def kblk(c):
        return pl.BlockSpec((seq, SEG), lambda b, i, c=c: (b, c))

    return pl.pallas_call(
        functools.partial(_mixer_a_kernel, seq=seq, topk=topk),
        out_shape=jax.ShapeDtypeStruct((batch * seq, MIX_WIDTH), F32),
        grid=(batch, nq),
        in_specs=[
            qblk(g0 + 0), kblk(g0 + 1), kblk(g0 + 2), qblk(g0 + 3), qblk(g0 + 10), kblk(g0 + 10),
            pl.BlockSpec((1, MIX_WIDTH), lambda b, i: (0, 0)),
            pl.BlockSpec((1, MIX_WIDTH), lambda b, i: (0, 0)),
        ],
        out_specs=pl.BlockSpec((t, MIX_WIDTH), lambda b, i: (b * nq + i, 0)),
        scratch_shapes=[
            pltpu.VMEM((seq, MIX_WIDTH), BF16),
            pltpu.VMEM((seq // t, MIX_WIDTH, t), BF16),
            pltpu.VMEM((seq, IDX_DIM), BF16),
            pltpu.VMEM((N_HEADS * t, MIX_WIDTH), BF16),
            pltpu.VMEM((IDX_HEADS, t, IDX_DIM), BF16),
            pltpu.VMEM((seq // t, t, t), F32),
            pltpu.VMEM((MIX_WIDTH, t), F32),
            pltpu.VMEM((8, t), F32),
        ],
        compiler_params=_cparams(("parallel", "arbitrary")),
        name="mixer_a",
    )(z, z, z, z, z, z, qg, kg)


def _mixer_d_kernel(sq_ref, sk_ref, sv_ref, o_ref, k_scr, vm_scr, qm_scr, acc_scr, carry_scr, *, seq):
    t = ATTN_T
    i = pl.program_id(1)
    lane_head = _lane_head((t, MIX_WIDTH))

    @pl.when(i == 0)
    def _prep_keys():
        def body(c, carry):
            r0 = pl.multiple_of(c * t, t)
            k_scr[pl.ds(r0, t), :] = sk_ref[pl.ds(r0, t), :].astype(BF16)
            v = sv_ref[pl.ds(r0, t), :].astype(BF16)
            for h in range(N_HEADS):
                vm_scr[h, pl.ds(r0, t), :] = jnp.where(lane_head == h, v, jnp.zeros_like(v))
            return carry

        lax.fori_loop(0, seq // t, body, 0)

    q = (sq_ref[...] * ATTN_SCALE).astype(BF16)
    for h in range(N_HEADS):
        qm_scr[h] = jnp.where(lane_head == h, q, jnp.zeros_like(q))
    acc_scr[...] = jnp.zeros((t, MIX_WIDTH), F32)
    carry_scr[...] = jnp.zeros((N_HEADS, t, LANES), F32)

    row = lax.broadcasted_iota(jnp.int32, (t, t), 0)
    col = lax.broadcasted_iota(jnp.int32, (t, t), 1)
    strict = col < row
    later = jnp.where(row > col, 1.0, 0.0).astype(BF16)

    def chunk(j, masked):
        r0 = pl.multiple_of(j * t, t)
        kc = k_scr[pl.ds(r0, t), :]
        pv = None
        cmin = None
        for h in range(N_HEADS):
            z = _dot_nt(qm_scr[h], kc)
            nl = jnp.maximum(z, 0.0) + jnp.log(1.0 + jnp.exp(-jnp.abs(z)))
            if masked:
                nl = jnp.where(strict, nl, 0.0)
            carry = carry_scr[h]
            tail = _split_dot(nl, later) + _rep2(carry)
            a = jnp.exp(z - nl - tail)
            if masked:
                a = jnp.where(strict, a, 0.0)
            d = _dot(a.astype(BF16), vm_scr[h, pl.ds(r0, t), :])
            pv = d if pv is None else pv + d
            carry = carry + jnp.sum(nl, axis=1, keepdims=True)
            carry_scr[h] = carry
            cmin = carry if cmin is None else jnp.minimum(cmin, carry)
        acc_scr[...] += pv
        return jnp.min(cmin)

    def cond(c):
        return jnp.logical_and(c[0] <= i, c[1] < SB_EXIT)

    def body(c):
        return c[0] + 1, chunk(i - c[0], False)

    lax.while_loop(cond, body, (jnp.int32(1), chunk(i, True)))
    o_ref[...] = acc_scr[...]


def _mixer_d(z, *, batch, seq):
    t = ATTN_T
    nq = seq // t
    g0 = z.shape[1] // SEG - N_SEGS

    return pl.pallas_call(
        functools.partial(_mixer_d_kernel, seq=seq),
        out_shape=jax.ShapeDtypeStruct((batch * seq, MIX_WIDTH), F32),
        grid=(batch, nq),
        in_specs=[
            pl.BlockSpec((t, SEG), lambda b, i: (b * nq + i, g0 + 7)),
            pl.BlockSpec((seq, SEG), lambda b, i: (b, g0 + 8)),
            pl.BlockSpec((seq, SEG), lambda b, i: (b, g0 + 9)),
        ],
        out_specs=pl.BlockSpec((t, MIX_WIDTH), lambda b, i: (b * nq + i, 0)),
        scratch_shapes=[
            pltpu.VMEM((seq, MIX_WIDTH), BF16),
            pltpu.VMEM((N_HEADS, seq, MIX_WIDTH), BF16),
            pltpu.VMEM((N_HEADS, t, MIX_WIDTH), BF16),
            pltpu.VMEM((t, MIX_WIDTH), F32),
            pltpu.VMEM((N_HEADS, t, LANES), F32),
        ],
        compiler_params=_cparams(("parallel", "arbitrary")),
        name="mixer_d",
    )(z, z, z)


def _merge_kernel(x_ref, oa_ref, od_ref, pin_ref, halo_ref, gu_ref, gv_ref, gl_ref,
                  poolw_ref, pscale_ref, gmn_ref, wcat_ref, gbias_ref, wbr_ref, wout_ref,
                  o_ref, ext_scr, *, seq):
    tm = MERGE_TM
    d = o_ref.shape[1]
    tile_in_seq = pl.program_id(0) % (seq // tm)

    pin = pin_ref[...]
    halo = halo_ref[...]
    ext_scr[0:POOL_HALO, :] = jnp.where(tile_in_seq == 0, jnp.zeros_like(halo), halo)
    ext_scr[POOL_HALO:POOL_HALO + tm, :] = pin
    lane_grp = lax.broadcasted_iota(jnp.int32, (1, MIX_WIDTH), 1) // (MIX_WIDTH // len(POOL_WINDOWS))
    wsum = jnp.zeros((tm, MIX_WIDTH), F32)
    run = pin
    for j in range(1, max(POOL_WINDOWS)):
        run = run + ext_scr[POOL_HALO - j:POOL_HALO - j + tm, :]
        if (j + 1) in POOL_WINDOWS:
            wsum = jnp.where(lane_grp == POOL_WINDOWS.index(j + 1), run, wsum)
    win = jnp.zeros((1, MIX_WIDTH), F32)
    for g, wlen in enumerate(POOL_WINDOWS):
        win = jnp.where(lane_grp == g, float(wlen), win)
    pos1 = (tile_in_seq * tm + lax.broadcasted_iota(jnp.int32, (tm, 1), 0) + 1).astype(F32)
    pooled = wsum / jnp.minimum(pos1, win) - pin
    o_b = _dot(pooled.astype(BF16), poolw_ref[...]) * pscale_ref[...]

    gu = _gelu_tanh(gu_ref[...])
    gv = _gelu_tanh(gv_ref[...])
    mu = jnp.mean(gv, axis=-1, keepdims=True)
    xc = gv - mu
    var = jnp.mean(xc * xc, axis=-1, keepdims=True)
    vb = (xc * lax.rsqrt(var + EPS) * gmn_ref[...]).astype(BF16)
    kw = GMLP_GROUPS * GMLP_CHUNK
    wr = lax.broadcasted_iota(jnp.int32, (GMLP_CHUNK, kw), 0)
    wc = lax.broadcasted_iota(jnp.int32, (GMLP_CHUNK, kw), 1)
    wcat = jnp.where((wc % GMLP_CHUNK) <= wr, wcat_ref[...], jnp.zeros((GMLP_CHUNK, kw), BF16))
    gr = lax.broadcasted_iota(jnp.int32, (kw, MIX_WIDTH), 0) // GMLP_CHUNK
    gc = lax.broadcasted_iota(jnp.int32, (kw, MIX_WIDTH), 1) // (MIX_WIDTH // GMLP_GROUPS)
    mixed = []
    for c in range(tm // GMLP_CHUNK):
        vc = vb[c * GMLP_CHUNK:(c + 1) * GMLP_CHUNK, :]
        vbd = jnp.where(gr == gc, jnp.concatenate([vc] * GMLP_GROUPS, axis=0),
                        jnp.zeros((kw, MIX_WIDTH), BF16))
        mixed.append(_dot(wcat, vbd) + gbias_ref[...])
    o_c = gu * jnp.concatenate(mixed, axis=0)

    merged = jnp.zeros((tm, d), F32)
    for n, o_n in enumerate((oa_ref[...], o_b, o_c, od_ref[...])):
        proj = _dot(o_n.astype(BF16), wbr_ref[n])
        merged = merged + _sigmoid(gl_ref[:, n * d:(n + 1) * d]) * proj
    o_ref[...] = x_ref[...] + _dot(merged.astype(BF16), wout_ref[...])


def _merge(x2, o_a, o_d, z, poolw_bd, pscale, gmn, wcat, gbias, wbr, wout, *, seq):
    n, d = x2.shape
    tm = MERGE_TM
    g0 = N_BRANCH * d // SEG

    def seg(c):
        return pl.BlockSpec((tm, SEG), lambda i, c=c: (i, c))

    def full(a):
        nd = a.ndim
        return pl.BlockSpec(a.shape, lambda i, nd=nd: (0,) * nd)

    halo_blocks = tm // POOL_HALO
    return pl.pallas_call(
        functools.partial(_merge_kernel, seq=seq),
        out_shape=jax.ShapeDtypeStruct((n, d), F32),
        grid=(n // tm,),
        in_specs=[
            pl.BlockSpec((tm, d), lambda i: (i, 0)),
            pl.BlockSpec((tm, MIX_WIDTH), lambda i: (i, 0)),
            pl.BlockSpec((tm, MIX_WIDTH), lambda i: (i, 0)),
            seg(g0 + 4),
            pl.BlockSpec((POOL_HALO, SEG),
                         lambda i: (jnp.maximum(i * halo_blocks - 1, 0), g0 + 4)),
            seg(g0 + 5), seg(g0 + 6),
            pl.BlockSpec((tm, N_BRANCH * d), lambda i: (i, 0)),
            full(poolw_bd), full(pscale), full(gmn), full(wcat), full(gbias), full(wbr), full(wout),
        ],
        out_specs=pl.BlockSpec((tm, d), lambda i: (i, 0)),
        scratch_shapes=[pltpu.VMEM((POOL_HALO + tm, MIX_WIDTH), F32)],
        compiler_params=_cparams(("parallel",)),
        name="merge",
    )(x2, o_a, o_d, z, z, z, z, z, poolw_bd, pscale, gmn, wcat, gbias, wbr, wout)


def _ffn_kernel(x_ref, g_ref, wg_ref, wu_ref, wd_ref, o_ref, h_scr, acc_scr):
    k = pl.program_id(1)

    @pl.when(k == 0)
    def _():
        x = x_ref[...]
        ms = jnp.mean(x * x, axis=-1, keepdims=True)
        h_scr[...] = (x * lax.rsqrt(ms + EPS) * g_ref[...]).astype(BF16)
        acc_scr[...] = jnp.zeros_like(acc_scr)

    h = h_scr[...]
    g = _dot(h, wg_ref[...])
    u = _dot(h, wu_ref[...])
    acc_scr[...] += _dot((g * _sigmoid(g) * u).astype(BF16), wd_ref[...])

    @pl.when(k == pl.num_programs(1) - 1)
    def _():
        o_ref[...] = x_ref[...] + acc_scr[...]


def _ffn(x2, gain, wg, wu, wd):
    n, d = x2.shape
    d_ff = wg.shape[1]
    return pl.pallas_call(
        _ffn_kernel,
        out_shape=jax.ShapeDtypeStruct((n, d), F32),
        grid=(n // FFN_TM, d_ff // FFN_TF),
        in_specs=[
            pl.BlockSpec((FFN_TM, d), lambda i, k: (i, 0)),
            pl.BlockSpec((1, d), lambda i, k: (0, 0)),
            pl.BlockSpec((d, FFN_TF), lambda i, k: (0, k)),
            pl.BlockSpec((d, FFN_TF), lambda i, k: (0, k)),
            pl.BlockSpec((FFN_TF, d), lambda i, k: (k, 0)),
        ],
        out_specs=pl.BlockSpec((FFN_TM, d), lambda i, k: (i, 0)),
        scratch_shapes=[pltpu.VMEM((FFN_TM, d), BF16), pltpu.VMEM((FFN_TM, d), F32)],
        compiler_params=_cparams(("parallel", "arbitrary")),
        name="ffn",
    )(x2, gain, wg, wu, wd)


def _prep_w_in(w_in_l, d):
    a_end = 3 * MIX_WIDTH + IDX_HEADS * IDX_DIM
    i_end = a_end + IDX_DIM + IDX_HEADS
    r_end = i_end + 6 * MIX_WIDTH
    pad = SEG - (IDX_DIM + IDX_HEADS)
    return jnp.concatenate(
        [w_in_l[:, r_end:], w_in_l[:, :a_end], w_in_l[:, i_end:r_end], w_in_l[:, a_end:i_end],
         jnp.zeros((d, pad), w_in_l.dtype)], axis=1).astype(BF16)


def kernel(x, mix_norm, w_in, attn_q_norm, attn_k_norm, pool_w, pool_scale, gmlp_norm, gmlp_w_s,
           gmlp_b, w_branch, w_out, ffn_norm, w_ffn_gate, w_ffn_up, w_ffn_down):
    batch, seq, d = x.shape
    depth = w_in.shape[0]
    assert seq % ATTN_T == 0 and seq % MERGE_TM == 0 and (batch * seq) % INPROJ_TM == 0
    x2 = x.reshape(batch * seq, d)
    for l in range(depth):
        w_l = _prep_w_in(w_in[l], d)
        z = _inproj(x2, mix_norm[l][None, :], w_l)
        qg = jnp.tile(attn_q_norm[l], N_HEADS)[None, :]
        kg = jnp.tile(attn_k_norm[l], N_HEADS)[None, :]
        o_a = _mixer_a(z, qg, kg, batch=batch, seq=seq)
        o_d = _mixer_d(z, batch=batch, seq=seq)
        poolw_bd = jax.scipy.linalg.block_diag(*[pool_w[l, g] for g in range(len(POOL_WINDOWS))]).astype(BF16)
        wcat = jnp.concatenate([gmlp_w_s[l, g] for g in range(GMLP_GROUPS)], axis=1).astype(BF16)
        gbias = jnp.repeat(gmlp_b[l].T, MIX_WIDTH // GMLP_GROUPS, axis=1)
        x2 = _merge(x2, o_a, o_d, z, poolw_bd, pool_scale[l][None, :], gmlp_norm[l][None, :], wcat,
                    gbias, w_branch[l].astype(BF16), w_out[l].astype(BF16), seq=seq)
        x2 = _ffn(x2, ffn_norm[l][None, :], w_ffn_gate[l].astype(BF16), w_ffn_up[l].astype(BF16),
                  w_ffn_down[l].astype(BF16))
    return x2.reshape(batch, seq, d)
```

```python
import functools

import jax
import jax.numpy as jnp
from jax import lax
from jax.experimental import pallas as pl
from jax.experimental.pallas import tpu as pltpu

F32 = jnp.float32
BF16 = jnp.bfloat16

EPS = 1e-6
N_BRANCH = 4
MIX_WIDTH = 256
HEAD_DIM = 64
N_HEADS = MIX_WIDTH // HEAD_DIM
IDX_HEADS = 8
IDX_DIM = 32
TOPK_MAX = 256
POOL_WINDOWS = (2, 4, 8, 16)
POOL_HALO = 16
GMLP_CHUNK = 128
GMLP_GROUPS = 4
ATTN_SCALE = HEAD_DIM ** -0.5

SEG = 256
N_SEGS = 11
LANES = 128

INPROJ_TM = 1024
INPROJ_TN = 768
MIXA_TQ = 512
MIXA_TK = 256
ATTN_T = 256
MERGE_TM = 512
FFN_TM = 512
FFN_TF = 1408

MASK_BIAS = -2e30
MAX_INIT = -1e30
BISECT_ITERS = 14
PEEL_ITERS = 8
FALLBACK_ITERS = 160
SB_EXIT = 105.0

VMEM_LIMIT = 56 * 1024 * 1024


def _cparams(sem):
    return pltpu.CompilerParams(dimension_semantics=sem, vmem_limit_bytes=VMEM_LIMIT)


def _dot(a, b):
    return jnp.dot(a, b, preferred_element_type=F32)


def _dot_nt(a, b):
    return lax.dot_general(a, b, (((1,), (1,)), ((), ())), preferred_element_type=F32)


def _sigmoid(x):
    return 1.0 / (1.0 + jnp.exp(-x))


def _gelu_tanh(x):
    c = 0.7978845608028654
    return 0.5 * x * (1.0 + jnp.tanh(c * (x + 0.044715 * (x * x * x))))


def _split_dot(x, ones_mat):
    hi = x.astype(BF16)
    lo = (x - hi.astype(F32)).astype(BF16)
    return _dot(hi, ones_mat) + _dot(lo, ones_mat)


def _head_blockdiag_ones():
    r = lax.broadcasted_iota(jnp.int32, (MIX_WIDTH, MIX_WIDTH), 0) // HEAD_DIM
    c = lax.broadcasted_iota(jnp.int32, (MIX_WIDTH, MIX_WIDTH), 1) // HEAD_DIM
    return jnp.where(r == c, 1.0, 0.0).astype(BF16)


def _head_rms_norm(x, gain):
    ms = _split_dot(x * x, _head_blockdiag_ones()) * (1.0 / HEAD_DIM)
    return x * lax.rsqrt(ms + EPS) * gain


def _lane_head(shape):
    return lax.broadcasted_iota(jnp.int32, shape, len(shape) - 1) // HEAD_DIM


def _rep2(a):
    return jnp.concatenate([a, a], axis=1)


def _inproj_kernel(x_ref, g_ref, w_ref, o_ref, h_scr):
    @pl.when(pl.program_id(1) == 0)
    def _():
        x = x_ref[...]
        ms = jnp.mean(x * x, axis=-1, keepdims=True)
        h_scr[...] = (x * lax.rsqrt(ms + EPS) * g_ref[...]).astype(BF16)

    o_ref[...] = _dot(h_scr[...], w_ref[...])


def _inproj(x2, gain, w):
    n, d = x2.shape
    d_in = w.shape[1]
    return pl.pallas_call(
        _inproj_kernel,
        out_shape=jax.ShapeDtypeStruct((n, d_in), F32),
        grid=(n // INPROJ_TM, d_in // INPROJ_TN),
        in_specs=[
            pl.BlockSpec((INPROJ_TM, d), lambda i, j: (i, 0)),
            pl.BlockSpec((1, d), lambda i, j: (0, 0)),
            pl.BlockSpec((d, INPROJ_TN), lambda i, j: (0, j)),
        ],
        out_specs=pl.BlockSpec((INPROJ_TM, INPROJ_TN), lambda i, j: (i, j)),
        scratch_shapes=[pltpu.VMEM((INPROJ_TM, d), BF16)],
        compiler_params=_cparams(("parallel", "arbitrary")),
        name="inproj",
    )(x2, gain, w)


def _mixer_a_kernel(aq_ref, ak_ref, av_ref, iq_ref, ikwq_ref, ikwk_ref, qg_ref, kg_ref, o_ref,
                    kn_scr, vt_scr, kidx_scr, qm_scr, qi_scr, sc_scr, acc_scr, j_scr, *, seq, topk):
    tq, tk = MIXA_TQ, MIXA_TK
    i = pl.program_id(1)
    n_full = i * (tq // tk)
    n_diag = tq // tk
    nch = n_full + n_diag
    kf = float(topk)
    inf = jnp.inf

    @pl.when(i == 0)
    def _prep_keys():
        def body(c, carry):
            r0 = pl.multiple_of(c * tk, tk)
            kn_scr[pl.ds(r0, tk), :] = _head_rms_norm(ak_ref[pl.ds(r0, tk), :], kg_ref[...]).astype(BF16)
            vt_scr[c] = av_ref[pl.ds(r0, tk), :].T.astype(BF16)
            kidx_scr[pl.ds(r0, tk), :] = ikwk_ref[pl.ds(r0, tk), 0:IDX_DIM].astype(BF16)
            return carry

        lax.fori_loop(0, seq // tk, body, 0)

    qn = (_head_rms_norm(aq_ref[...], qg_ref[...]) * ATTN_SCALE).astype(BF16)
    lane_head = _lane_head((tq, MIX_WIDTH))
    for h in range(N_HEADS):
        qm_scr[pl.ds(h * tq, tq), :] = jnp.where(lane_head == h, qn, jnp.zeros_like(qn))
    qi = iq_ref[...].astype(BF16)
    for h in range(IDX_HEADS):
        qi_scr[h] = qi[:, IDX_DIM * h:IDX_DIM * (h + 1)]
    w_t = ikwq_ref[:, 0:LANES].T[IDX_DIM:IDX_DIM + IDX_HEADS, :]

    key_i = lax.broadcasted_iota(jnp.int32, (tk, tq), 0)
    qry_i = lax.broadcasted_iota(jnp.int32, (tk, tq), 1)
    key_f = key_i.astype(F32)

    def colmax(a):
        return jnp.max(a, axis=0, keepdims=True)

    def colmin(a):
        return jnp.min(a, axis=0, keepdims=True)

    def colsum(a):
        return jnp.sum(a, axis=0, keepdims=True)

    def fold8(a):
        return jnp.sum(a.reshape(tk // 8, 8, tq), axis=0)

    def score_chunk(j):
        r0 = pl.multiple_of(j * tk, tk)
        kc = kidx_scr[pl.ds(r0, tk), :]
        acc = jnp.zeros((tk, tq), F32)
        for h in range(IDX_HEADS):
            acc = acc + w_t[h:h + 1, :] * jnp.maximum(_dot_nt(kc, qi_scr[h]), 0.0)
        return acc

    def score_body(j, carry):
        mx, mn = carry
        acc = score_chunk(j)
        sc_scr[j] = acc
        return jnp.maximum(mx, colmax(acc)), jnp.minimum(mn, colmin(acc))

    rmax, rmin = lax.fori_loop(0, n_full, score_body,
                               (jnp.full((1, tq), -inf, F32), jnp.full((1, tq), inf, F32)))
    for d in range(n_diag):
        causal = key_i + d * tk <= qry_i
        acc = score_chunk(n_full + d)
        diag = jnp.where(causal, acc, -inf)
        sc_scr[n_full + d] = diag
        rmax = jnp.maximum(rmax, colmax(diag))
        rmin = jnp.minimum(rmin, colmin(jnp.where(causal, acc, inf)))

    def count_ge(thr):
        def body(j, cnt):
            return cnt + fold8(jnp.where(sc_scr[j] >= thr, 1.0, 0.0))

        return colsum(lax.fori_loop(0, nch, body, jnp.zeros((8, tq), F32)))

    def max_below(thr):
        def body(j, m):
            x = sc_scr[j]
            return jnp.maximum(m, colmax(jnp.where(x < thr, x, -inf)))

        return lax.fori_loop(0, nch, body, jnp.full((1, tq), -inf, F32))

    def probe(state, mid, closes):
        lo, hi, cl, ch, fin = state
        cnt = count_ge(mid)
        act = fin < 0.5
        ge = cnt >= kf
        take_lo = jnp.logical_and(act, ge)
        take_hi = jnp.logical_and(act, jnp.logical_not(ge))
        lo = jnp.where(take_lo, mid, lo)
        cl = jnp.where(take_lo, cnt, cl)
        hi = jnp.where(take_hi, mid, hi)
        ch = jnp.where(take_hi, cnt, ch)
        done = cl <= kf
        if closes:
            done = jnp.logical_or(done, take_lo)
        fin = jnp.where(done, 1.0, fin)
        return lo, hi, cl, ch, fin

    def bisect_step(state):
        lo, hi = state[0], state[1]
        return probe(state, lo + 0.5 * (hi - lo), False)

    def peel_step(state):
        return probe(state, max_below(state[1]), True)

    def run(step, limit, state):
        def cond(c):
            return jnp.logical_and(c[0] < limit, c[1] < 0.5)

        def body(c):
            new = step(c[2])
            return c[0] + 1, jnp.min(new[4]), new

        return lax.while_loop(cond, body, (jnp.int32(0), jnp.min(state[4]), state))[2]

    qry_pos = lax.broadcasted_iota(jnp.int32, (1, tq), 1)
    n_valid = (i * tq + qry_pos + 1).astype(F32)
    c_top = count_ge(rmax)
    all_in = n_valid <= kf
    top_tie = jnp.logical_and(c_top >= kf, jnp.logical_not(all_in))
    state = (jnp.where(top_tie, rmax, rmin),
             jnp.where(top_tie, inf, rmax),
             jnp.where(top_tie, c_top, n_valid),
             jnp.where(top_tie, 0.0, c_top),
             jnp.where(jnp.logical_or(all_in, top_tie), 1.0, 0.0))
    state = lax.fori_loop(0, BISECT_ITERS, lambda _, s: bisect_step(s), state)
    state = run(peel_step, PEEL_ITERS, state)
    state = run(lambda s: peel_step(bisect_step(s)), FALLBACK_ITERS, state)
    lo, hi, cl, ch, _ = state

    j_scr[...] = jnp.full((8, tq), float(seq), F32)

    @pl.when(jnp.max(cl) > kf)
    def _ties():
        need = kf - ch

        def count_band_le(jm):
            def body(j, cnt):
                x = sc_scr[j]
                kidx = key_f + (j * tk).astype(F32)
                m = jnp.where(x >= lo, jnp.where(x >= hi, 0.0, jnp.where(kidx <= jm, 1.0, 0.0)), 0.0)
                return cnt + colsum(m)

            return lax.fori_loop(0, nch, body, jnp.zeros((1, tq), F32))

        def search(_, c):
            jlo, jhi = c
            jm = jnp.floor(0.5 * (jlo + jhi))
            ok = count_band_le(jm) >= need
            return jnp.where(ok, jlo, jm), jnp.where(ok, jm, jhi)

        _, jhi = lax.fori_loop(0, seq.bit_length(), search,
                               (jnp.full((1, tq), -1.0, F32), jnp.full((1, tq), float(seq - 1), F32)))
        j_scr[...] = jnp.broadcast_to(jhi, (8, tq))

    jsel = j_scr[0:1, :]

    def bias_body(j, carry):
        x = sc_scr[j]
        kidx = key_f + (j * tk).astype(F32)
        sc_scr[j] = jnp.where(
            x >= lo, jnp.where(x >= hi, 0.0, jnp.where(kidx <= jsel, 0.0, MASK_BIAS)), MASK_BIAS)
        return carry

    lax.fori_loop(0, nch, bias_body, 0)

    acc_scr[...] = jnp.zeros((MIX_WIDTH, tq), F32)

    def attn_body(j, carry):
        ms, ls = carry
        r0 = pl.multiple_of(j * tk, tk)
        bias = sc_scr[j]
        logits = _dot_nt(kn_scr[pl.ds(r0, tk), :], qm_scr[...])
        new_ms, new_ls, alphas, pvs = [], [], [], []
        for h in range(N_HEADS):
            s = logits[:, h * tq:(h + 1) * tq] + bias
            m_new = jnp.maximum(ms[h], colmax(s))
            alphas.append(jnp.exp(ms[h] - m_new))
            p = jnp.exp(s - m_new)
            new_ls.append(alphas[h] * ls[h] + colsum(p))
            pvs.append(_dot(vt_scr[j, pl.ds(HEAD_DIM * h, HEAD_DIM), :], p.astype(BF16)))
            new_ms.append(m_new)
        for h in range(N_HEADS):
            rows = pl.ds(HEAD_DIM * h, HEAD_DIM)
            acc_scr[rows, :] = alphas[h] * acc_scr[rows, :] + pvs[h]
        return tuple(new_ms), tuple(new_ls)

    m0 = tuple(jnp.full((1, tq), MAX_INIT, F32) for _ in range(N_HEADS))
    l0 = tuple(jnp.zeros((1, tq), F32) for _ in range(N_HEADS))
    _, ls = lax.fori_loop(0, nch, attn_body, (m0, l0))
    for h in range(N_HEADS):
        rows = pl.ds(HEAD_DIM * h, HEAD_DIM)
        acc_scr[rows, :] = acc_scr[rows, :] / ls[h]
    o_ref[...] = acc_scr[...].T


def _mixer_a(z, qg, kg, *, batch, seq):
    tq, tk = MIXA_TQ, MIXA_TK
    nq = seq // tq
    topk = min(TOPK_MAX, seq // 4)
    g0 = z.shape[1] // SEG - N_SEGS

    def qblk(c):
        return pl.BlockSpec((tq, SEG), lambda b, i, c=c: (b * nq + i, c))

    def kblk(c):
        return pl.BlockSpec((seq, SEG), lambda b, i, c=c: (b, c))

    return pl.pallas_call(
        functools.partial(_mixer_a_kernel, seq=seq, topk=topk),
        out_shape=jax.ShapeDtypeStruct((batch * seq, MIX_WIDTH), F32),
        grid=(batch, nq),
        in_specs=[
            qblk(g0 + 0), kblk(g0 + 1), kblk(g0 + 2), qblk(g0 + 3), qblk(g0 + 10), kblk(g0 + 10),
            pl.BlockSpec((1, MIX_WIDTH), lambda b, i: (0, 0)),
            pl.BlockSpec((1, MIX_WIDTH), lambda b, i: (0, 0)),
        ],
        out_specs=pl.BlockSpec((tq, MIX_WIDTH), lambda b, i: (b * nq + i, 0)),
        scratch_shapes=[
            pltpu.VMEM((seq, MIX_WIDTH), BF16),
            pltpu.VMEM((seq // tk, MIX_WIDTH, tk), BF16),
            pltpu.VMEM((seq, IDX_DIM), BF16),
            pltpu.VMEM((N_HEADS * tq, MIX_WIDTH), BF16),
            pltpu.VMEM((IDX_HEADS, tq, IDX_DIM), BF16),
            pltpu.VMEM((seq // tk, tk, tq), F32),
            pltpu.VMEM((MIX_WIDTH, tq), F32),
            pltpu.VMEM((8, tq), F32),
        ],
        compiler_params=_cparams(("parallel", "arbitrary")),
        name="mixer_a",
    )(z, z, z, z, z, z, qg, kg)


def _mixer_d_kernel(sq_ref, sk_ref, sv_ref, o_ref, k_scr, vm_scr, qm_scr, acc_scr, carry_scr, *, seq):
    t = ATTN_T
    i = pl.program_id(1)
    lane_head = _lane_head((t, MIX_WIDTH))

    @pl.when(i == 0)
    def _prep_keys():
        def body(c, carry):
            r0 = pl.multiple_of(c * t, t)
            k_scr[pl.ds(r0, t), :] = sk_ref[pl.ds(r0, t), :].astype(BF16)
            v = sv_ref[pl.ds(r0, t), :].astype(BF16)
            for h in range(N_HEADS):
                vm_scr[h, pl.ds(r0, t), :] = jnp.where(lane_head == h, v, jnp.zeros_like(v))
            return carry

        lax.fori_loop(0, seq // t, body, 0)

    q = (sq_ref[...] * ATTN_SCALE).astype(BF16)
    for h in range(N_HEADS):
        qm_scr[pl.ds(h * t, t), :] = jnp.where(lane_head == h, q, jnp.zeros_like(q))
    acc_scr[...] = jnp.zeros((t, MIX_WIDTH), F32)
    carry_scr[...] = jnp.zeros((N_HEADS, t, LANES), F32)

    row = lax.broadcasted_iota(jnp.int32, (t, t), 0)
    col = lax.broadcasted_iota(jnp.int32, (t, t), 1)
    strict = col < row
    later = jnp.where(row > col, 1.0, 0.0).astype(BF16)

    def chunk(j, masked):
        r0 = pl.multiple_of(j * t, t)
        z_all = _dot_nt(qm_scr[...], k_scr[pl.ds(r0, t), :])
        pv = None
        cmin = None
        for h in range(N_HEADS):
            z = z_all[h * t:(h + 1) * t, :]
            nl = jnp.maximum(z, 0.0) + jnp.log(1.0 + jnp.exp(-jnp.abs(z)))
            if masked:
                nl = jnp.where(strict, nl, 0.0)
            carry = carry_scr[h]
            tail = _split_dot(nl, later) + _rep2(carry)
            a = jnp.exp(z - nl - tail)
            if masked:
                a = jnp.where(strict, a, 0.0)
            d = _dot(a.astype(BF16), vm_scr[h, pl.ds(r0, t), :])
            pv = d if pv is None else pv + d
            carry = carry + jnp.sum(nl, axis=1, keepdims=True)
            carry_scr[h] = carry
            cmin = carry if cmin is None else jnp.minimum(cmin, carry)
        acc_scr[...] += pv
        return jnp.min(cmin)

    def cond(c):
        return jnp.logical_and(c[0] <= i, c[1] < SB_EXIT)

    def body(c):
        return c[0] + 1, chunk(i - c[0], False)

    lax.while_loop(cond, body, (jnp.int32(1), chunk(i, True)))
    o_ref[...] = acc_scr[...]


def _mixer_d(z, *, batch, seq):
    t = ATTN_T
    nq = seq // t
    g0 = z.shape[1] // SEG - N_SEGS

    return pl.pallas_call(
        functools.partial(_mixer_d_kernel, seq=seq),
        out_shape=jax.ShapeDtypeStruct((batch * seq, MIX_WIDTH), F32),
        grid=(batch, nq),
        in_specs=[
            pl.BlockSpec((t, SEG), lambda b, i: (b * nq + i, g0 + 7)),
            pl.BlockSpec((seq, SEG), lambda b, i: (b, g0 + 8)),
            pl.BlockSpec((seq, SEG), lambda b, i: (b, g0 + 9)),
        ],
        out_specs=pl.BlockSpec((t, MIX_WIDTH), lambda b, i: (b * nq + i, 0)),
        scratch_shapes=[
            pltpu.VMEM((seq, MIX_WIDTH), BF16),
            pltpu.VMEM((N_HEADS, seq, MIX_WIDTH), BF16),
            pltpu.VMEM((N_HEADS * t, MIX_WIDTH), BF16),
            pltpu.VMEM((t, MIX_WIDTH), F32),
            pltpu.VMEM((N_HEADS, t, LANES), F32),
        ],
        compiler_params=_cparams(("parallel", "arbitrary")),
        name="mixer_d",
    )(z, z, z)


def _merge_kernel(x_ref, oa_ref, od_ref, pin_ref, halo_ref, gu_ref, gv_ref, gl_ref,
                  poolw_ref, pscale_ref, gmn_ref, wcat_ref, gbias_ref, wbr_ref, wout_ref,
                  o_ref, ext_scr, *, seq):
    tm = MERGE_TM
    d = o_ref.shape[1]
    tile_in_seq = pl.program_id(0) % (seq // tm)

    pin = pin_ref[...]
    halo = halo_ref[...]
    ext_scr[0:POOL_HALO, :] = jnp.where(tile_in_seq == 0, jnp.zeros_like(halo), halo)
    ext_scr[POOL_HALO:POOL_HALO + tm, :] = pin
    lane_grp = lax.broadcasted_iota(jnp.int32, (1, MIX_WIDTH), 1) // (MIX_WIDTH // len(POOL_WINDOWS))
    wsum = jnp.zeros((tm, MIX_WIDTH), F32)
    run = pin
    for j in range(1, max(POOL_WINDOWS)):
        run = run + ext_scr[POOL_HALO - j:POOL_HALO - j + tm, :]
        if (j + 1) in POOL_WINDOWS:
            wsum = jnp.where(lane_grp == POOL_WINDOWS.index(j + 1), run, wsum)
    win = jnp.zeros((1, MIX_WIDTH), F32)
    for g, wlen in enumerate(POOL_WINDOWS):
        win = jnp.where(lane_grp == g, float(wlen), win)
    pos1 = (tile_in_seq * tm + lax.broadcasted_iota(jnp.int32, (tm, 1), 0) + 1).astype(F32)
    pooled = wsum / jnp.minimum(pos1, win) - pin
    o_b = _dot(pooled.astype(BF16), poolw_ref[...]) * pscale_ref[...]

    gu = _gelu_tanh(gu_ref[...])
    gv = _gelu_tanh(gv_ref[...])
    mu = jnp.mean(gv, axis=-1, keepdims=True)
    xc = gv - mu
    var = jnp.mean(xc * xc, axis=-1, keepdims=True)
    vb = (xc * lax.rsqrt(var + EPS) * gmn_ref[...]).astype(BF16)
    kw = GMLP_GROUPS * GMLP_CHUNK
    wr = lax.broadcasted_iota(jnp.int32, (GMLP_CHUNK, kw), 0)
    wc = lax.broadcasted_iota(jnp.int32, (GMLP_CHUNK, kw), 1)
    wcat = jnp.where((wc % GMLP_CHUNK) <= wr, wcat_ref[...], jnp.zeros((GMLP_CHUNK, kw), BF16))
    gr = lax.broadcasted_iota(jnp.int32, (kw, MIX_WIDTH), 0) // GMLP_CHUNK
    gc = lax.broadcasted_iota(jnp.int32, (kw, MIX_WIDTH), 1) // (MIX_WIDTH // GMLP_GROUPS)
    mixed = []
    for c in range(tm // GMLP_CHUNK):
        vc = vb[c * GMLP_CHUNK:(c + 1) * GMLP_CHUNK, :]
        vbd = jnp.where(gr == gc, jnp.concatenate([vc] * GMLP_GROUPS, axis=0),
                        jnp.zeros((kw, MIX_WIDTH), BF16))
        mixed.append(_dot(wcat, vbd) + gbias_ref[...])
    o_c = gu * jnp.concatenate(mixed, axis=0)

    merged = jnp.zeros((tm, d), F32)
    for n, o_n in enumerate((oa_ref[...], o_b, o_c, od_ref[...])):
        proj = _dot(o_n.astype(BF16), wbr_ref[n])
        merged = merged + _sigmoid(gl_ref[:, n * d:(n + 1) * d]) * proj
    o_ref[...] = x_ref[...] + _dot(merged.astype(BF16), wout_ref[...])


def _merge(x2, o_a, o_d, z, poolw_bd, pscale, gmn, wcat, gbias, wbr, wout, *, seq):
    n, d = x2.shape
    tm = MERGE_TM
    g0 = N_BRANCH * d // SEG

    def seg(c):
        return pl.BlockSpec((tm, SEG), lambda i, c=c: (i, c))

    def full(a):
        nd = a.ndim
        return pl.BlockSpec(a.shape, lambda i, nd=nd: (0,) * nd)

    halo_blocks = tm // POOL_HALO
    return pl.pallas_call(
        functools.partial(_merge_kernel, seq=seq),
        out_shape=jax.ShapeDtypeStruct((n, d), F32),
        grid=(n // tm,),
        in_specs=[
            pl.BlockSpec((tm, d), lambda i: (i, 0)),
            pl.BlockSpec((tm, MIX_WIDTH), lambda i: (i, 0)),
            pl.BlockSpec((tm, MIX_WIDTH), lambda i: (i, 0)),
            seg(g0 + 4),
            pl.BlockSpec((POOL_HALO, SEG),
                         lambda i: (jnp.maximum(i * halo_blocks - 1, 0), g0 + 4)),
            seg(g0 + 5), seg(g0 + 6),
            pl.BlockSpec((tm, N_BRANCH * d), lambda i: (i, 0)),
            full(poolw_bd), full(pscale), full(gmn), full(wcat), full(gbias), full(wbr), full(wout),
        ],
        out_specs=pl.BlockSpec((tm, d), lambda i: (i, 0)),
        scratch_shapes=[pltpu.VMEM((POOL_HALO + tm, MIX_WIDTH), F32)],
        compiler_params=_cparams(("parallel",)),
        name="merge",
    )(x2, o_a, o_d, z, z, z, z, z, poolw_bd, pscale, gmn, wcat, gbias, wbr, wout)


def _ffn_kernel(x_ref, g_ref, wg_ref, wu_ref, wd_ref, o_ref, h_scr, acc_scr):
    k = pl.program_id(1)

    @pl.when(k == 0)
    def _():
        x = x_ref[...]
        ms = jnp.mean(x * x, axis=-1, keepdims=True)
        h_scr[...] = (x * lax.rsqrt(ms + EPS) * g_ref[...]).astype(BF16)
        acc_scr[...] = jnp.zeros_like(acc_scr)

    h = h_scr[...]
    g = _dot(h, wg_ref[...])
    u = _dot(h, wu_ref[...])
    acc_scr[...] += _dot((g * _sigmoid(g) * u).astype(BF16), wd_ref[...])

    @pl.when(k == pl.num_programs(1) - 1)
    def _():
        o_ref[...] = x_ref[...] + acc_scr[...]


def _ffn(x2, gain, wg, wu, wd):
    n, d = x2.shape
    d_ff = wg.shape[1]
    return pl.pallas_call(
        _ffn_kernel,
        out_shape=jax.ShapeDtypeStruct((n, d), F32),
        grid=(n // FFN_TM, d_ff // FFN_TF),
        in_specs=[
            pl.BlockSpec((FFN_TM, d), lambda i, k: (i, 0)),
            pl.BlockSpec((1, d), lambda i, k: (0, 0)),
            pl.BlockSpec((d, FFN_TF), lambda i, k: (0, k)),
            pl.BlockSpec((d, FFN_TF), lambda i, k: (0, k)),
            pl.BlockSpec((FFN_TF, d), lambda i, k: (k, 0)),
        ],
        out_specs=pl.BlockSpec((FFN_TM, d), lambda i, k: (i, 0)),
        scratch_shapes=[pltpu.VMEM((FFN_TM, d), BF16), pltpu.VMEM((FFN_TM, d), F32)],
        compiler_params=_cparams(("parallel", "arbitrary")),
        name="ffn",
    )(x2, gain, wg, wu, wd)


def _prep_w_in(w_in_l, d):
    a_end = 3 * MIX_WIDTH + IDX_HEADS * IDX_DIM
    i_end = a_end + IDX_DIM + IDX_HEADS
    r_end = i_end + 6 * MIX_WIDTH
    pad = SEG - (IDX_DIM + IDX_HEADS)
    return jnp.concatenate(
        [w_in_l[:, r_end:], w_in_l[:, :a_end], w_in_l[:, i_end:r_end], w_in_l[:, a_end:i_end],
         jnp.zeros((d, pad), w_in_l.dtype)], axis=1).astype(BF16)


def kernel(x, mix_norm, w_in, attn_q_norm, attn_k_norm, pool_w, pool_scale, gmlp_norm, gmlp_w_s,
           gmlp_b, w_branch, w_out, ffn_norm, w_ffn_gate, w_ffn_up, w_ffn_down):
    batch, seq, d = x.shape
    depth = w_in.shape[0]
    assert seq % ATTN_T == 0 and seq % MIXA_TQ == 0 and seq % MERGE_TM == 0 and (batch * seq) % INPROJ_TM == 0
    x2 = x.reshape(batch * seq, d)
    for l in range(depth):
        w_l = _prep_w_in(w_in[l], d)
        z = _inproj(x2, mix_norm[l][None, :], w_l)
        qg = jnp.tile(attn_q_norm[l], N_HEADS)[None, :]
        kg = jnp.tile(attn_k_norm[l], N_HEADS)[None, :]
        o_a = _mixer_a(z, qg, kg, batch=batch, seq=seq)
        o_d = _mixer_d(z, batch=batch, seq=seq)
        poolw_bd = jax.scipy.linalg.block_diag(*[pool_w[l, g] for g in range(len(POOL_WINDOWS))]).astype(BF16)
        wcat = jnp.concatenate([gmlp_w_s[l, g] for g in range(GMLP_GROUPS)], axis=1).astype(BF16)
        gbias = jnp.repeat(gmlp_b[l].T, MIX_WIDTH // GMLP_GROUPS, axis=1)
        x2 = _merge(x2, o_a, o_d, z, poolw_bd, pool_scale[l][None, :], gmlp_norm[l][None, :], wcat,
                    gbias, w_branch[l].astype(BF16), w_out[l].astype(BF16), seq=seq)
        x2 = _ffn(x2, ffn_norm[l][None, :], w_ffn_gate[l].astype(BF16), w_ffn_up[l].astype(BF16),
                  w_ffn_down[l].astype(BF16))
    return x2.reshape(batch, seq, d)
```

```python
import functools

import jax
import jax.numpy as jnp
from jax import lax
from jax.experimental import pallas as pl
from jax.experimental.pallas import tpu as pltpu

F32 = jnp.float32
BF16 = jnp.bfloat16

EPS = 1e-6
N_BRANCH = 4
MIX_WIDTH = 256
HEAD_DIM = 64
N_HEADS = MIX_WIDTH // HEAD_DIM
IDX_HEADS = 8
IDX_DIM = 32
TOPK_MAX = 256
POOL_WINDOWS = (2, 4, 8, 16)
POOL_HALO = 16
GMLP_CHUNK = 128
GMLP_GROUPS = 4
ATTN_SCALE = HEAD_DIM ** -0.5

SEG = 256
N_SEGS = 11
LANES = 128

INPROJ_TM = 1024
INPROJ_TN = 768
MIXA_TQ = 512
MIXA_TK = 256
ATTN_T = 256
MERGE_TM = 512
FFN_TM = 512
FFN_TF = 1408

MASK_BIAS = -2e30
MAX_INIT = -1e30
BISECT_ITERS = 14
PEEL_ITERS = 8
FALLBACK_ITERS = 160
SB_EXIT = 105.0

VMEM_LIMIT = 56 * 1024 * 1024


def _cparams(sem):
    return pltpu.CompilerParams(dimension_semantics=sem, vmem_limit_bytes=VMEM_LIMIT)


def _dot(a, b):
    return jnp.dot(a, b, preferred_element_type=F32)


def _dot_nt(a, b):
    return lax.dot_general(a, b, (((1,), (1,)), ((), ())), preferred_element_type=F32)


def _sigmoid(x):
    return 1.0 / (1.0 + jnp.exp(-x))


def _gelu_tanh(x):
    c = 0.7978845608028654
    return 0.5 * x * (1.0 + jnp.tanh(c * (x + 0.044715 * (x * x * x))))


def _split_dot(x, ones_mat):
    hi = x.astype(BF16)
    lo = (x - hi.astype(F32)).astype(BF16)
    return _dot(hi, ones_mat) + _dot(lo, ones_mat)


def _head_blockdiag_ones():
    r = lax.broadcasted_iota(jnp.int32, (MIX_WIDTH, MIX_WIDTH), 0) // HEAD_DIM
    c = lax.broadcasted_iota(jnp.int32, (MIX_WIDTH, MIX_WIDTH), 1) // HEAD_DIM
    return jnp.where(r == c, 1.0, 0.0).astype(BF16)


def _head_rms_norm(x, gain):
    ms = _split_dot(x * x, _head_blockdiag_ones()) * (1.0 / HEAD_DIM)
    return x * lax.rsqrt(ms + EPS) * gain


def _lane_head(shape):
    return lax.broadcasted_iota(jnp.int32, shape, len(shape) - 1) // HEAD_DIM


def _rep2(a):
    return jnp.concatenate([a, a], axis=1)


def _inproj_kernel(x_ref, g_ref, w_ref, o_ref, h_scr):
    @pl.when(pl.program_id(1) == 0)
    def _():
        x = x_ref[...]
        ms = jnp.mean(x * x, axis=-1, keepdims=True)
        h_scr[...] = (x * lax.rsqrt(ms + EPS) * g_ref[...]).astype(BF16)

    o_ref[...] = _dot(h_scr[...], w_ref[...])


def _inproj(x2, gain, w):
    n, d = x2.shape
    d_in = w.shape[1]
    return pl.pallas_call(
        _inproj_kernel,
        out_shape=jax.ShapeDtypeStruct((n, d_in), F32),
        grid=(n // INPROJ_TM, d_in // INPROJ_TN),
        in_specs=[
            pl.BlockSpec((INPROJ_TM, d), lambda i, j: (i, 0)),
            pl.BlockSpec((1, d), lambda i, j: (0, 0)),
            pl.BlockSpec((d, INPROJ_TN), lambda i, j: (0, j)),
        ],
        out_specs=pl.BlockSpec((INPROJ_TM, INPROJ_TN), lambda i, j: (i, j)),
        scratch_shapes=[pltpu.VMEM((INPROJ_TM, d), BF16)],
        compiler_params=_cparams(("parallel", "arbitrary")),
        name="inproj",
    )(x2, gain, w)


def _mixer_a_kernel(aq_ref, ak_ref, av_ref, iq_ref, ikwq_ref, ikwk_ref, qg_ref, kg_ref, o_ref,
                    kn_scr, vt_scr, kidx_scr, qm_scr, qi_scr, sc_scr, acc_scr, *, seq, topk):
    tq, tk = MIXA_TQ, MIXA_TK
    i = pl.program_id(1)
    n_full = i * (tq // tk)
    n_diag = tq // tk
    nch = n_full + n_diag
    kf = float(topk)
    inf = jnp.inf

    @pl.when(i == 0)
    def _prep_keys():
        def body(c, carry):
            r0 = pl.multiple_of(c * tk, tk)
            kn_scr[pl.ds(r0, tk), :] = _head_rms_norm(ak_ref[pl.ds(r0, tk), :], kg_ref[...]).astype(BF16)
            vt_scr[c] = av_ref[pl.ds(r0, tk), :].T.astype(BF16)
            kidx_scr[pl.ds(r0, tk), :] = ikwk_ref[pl.ds(r0, tk), 0:IDX_DIM].astype(BF16)
            return carry

        lax.fori_loop(0, seq // tk, body, 0)

    qn = (_head_rms_norm(aq_ref[...], qg_ref[...]) * ATTN_SCALE).astype(BF16)
    lane_head = _lane_head((tq, MIX_WIDTH))
    for h in range(N_HEADS):
        qm_scr[pl.ds(h * tq, tq), :] = jnp.where(lane_head == h, qn, jnp.zeros_like(qn))
    qi = iq_ref[...].astype(BF16)
    for h in range(IDX_HEADS):
        qi_scr[h] = qi[:, IDX_DIM * h:IDX_DIM * (h + 1)]
    w_t = ikwq_ref[:, 0:LANES].T[IDX_DIM:IDX_DIM + IDX_HEADS, :]

    key_i = lax.broadcasted_iota(jnp.int32, (tk, tq), 0)
    qry_i = lax.broadcasted_iota(jnp.int32, (tk, tq), 1)

    def colmax(a):
        return jnp.max(a, axis=0, keepdims=True)

    def colmin(a):
        return jnp.min(a, axis=0, keepdims=True)

    def colsum(a):
        return jnp.sum(a, axis=0, keepdims=True)

    def fold8(a):
        return jnp.sum(a.reshape(tk // 8, 8, tq), axis=0)

    def score_chunk(j):
        r0 = pl.multiple_of(j * tk, tk)
        kc = kidx_scr[pl.ds(r0, tk), :]
        acc = jnp.zeros((tk, tq), F32)
        for h in range(IDX_HEADS):
            acc = acc + w_t[h:h + 1, :] * jnp.maximum(_dot_nt(kc, qi_scr[h]), 0.0)
        return acc

    def score_body(m, carry):
        mx, mn = carry
        for d in range(n_diag):
            j = m * n_diag + d
            acc = score_chunk(j)
            sc_scr[j] = acc
            mx = jnp.maximum(mx, colmax(acc))
            mn = jnp.minimum(mn, colmin(acc))
        return mx, mn

    rmax, rmin = lax.fori_loop(0, i, score_body,
                               (jnp.full((1, tq), -inf, F32), jnp.full((1, tq), inf, F32)))
    for d in range(n_diag):
        causal = key_i + d * tk <= qry_i
        acc = score_chunk(n_full + d)
        diag = jnp.where(causal, acc, -inf)
        sc_scr[n_full + d] = diag
        rmax = jnp.maximum(rmax, colmax(diag))
        rmin = jnp.minimum(rmin, colmin(jnp.where(causal, acc, inf)))

    def count_ge(thr):
        def body(j, cnt):
            return cnt + fold8(jnp.where(sc_scr[j] >= thr, 1.0, 0.0))

        return colsum(lax.fori_loop(0, nch, body, jnp.zeros((8, tq), F32)))

    def max_below(thr):
        def body(j, m):
            x = sc_scr[j]
            return jnp.maximum(m, colmax(jnp.where(x < thr, x, -inf)))

        return lax.fori_loop(0, nch, body, jnp.full((1, tq), -inf, F32))

    def probe(state, mid, closes):
        lo, hi, cl, ch, fin = state
        cnt = count_ge(mid)
        act = fin < 0.5
        ge = cnt >= kf
        take_lo = jnp.logical_and(act, ge)
        take_hi = jnp.logical_and(act, jnp.logical_not(ge))
        lo = jnp.where(take_lo, mid, lo)
        cl = jnp.where(take_lo, cnt, cl)
        hi = jnp.where(take_hi, mid, hi)
        ch = jnp.where(take_hi, cnt, ch)
        done = cl <= kf
        if closes:
            done = jnp.logical_or(done, take_lo)
        fin = jnp.where(done, 1.0, fin)
        return lo, hi, cl, ch, fin

    def bisect_step(state):
        lo, hi = state[0], state[1]
        return probe(state, lo + 0.5 * (hi - lo), False)

    def peel_step(state):
        return probe(state, max_below(state[1]), True)

    def run(step, limit, state):
        def cond(c):
            return jnp.logical_and(c[0] < limit, c[1] < 0.5)

        def body(c):
            new = step(c[2])
            return c[0] + 1, jnp.min(new[4]), new

        return lax.while_loop(cond, body, (jnp.int32(0), jnp.min(state[4]), state))[2]

    qry_pos = lax.broadcasted_iota(jnp.int32, (1, tq), 1)
    n_valid = (i * tq + qry_pos + 1).astype(F32)
    c_top = count_ge(rmax)
    all_in = n_valid <= kf
    top_tie = jnp.logical_and(c_top >= kf, jnp.logical_not(all_in))
    state = (jnp.where(top_tie, rmax, rmin),
             jnp.where(top_tie, inf, rmax),
             jnp.where(top_tie, c_top, n_valid),
             jnp.where(top_tie, 0.0, c_top),
             jnp.where(jnp.logical_or(all_in, top_tie), 1.0, 0.0))
    state = lax.fori_loop(0, BISECT_ITERS, lambda _, s: bisect_step(s), state)
    state = run(peel_step, PEEL_ITERS, state)
    state = run(lambda s: peel_step(bisect_step(s)), FALLBACK_ITERS, state)
    lo, hi, cl, ch, _ = state

    any_tie = jnp.max(cl) > kf

    @pl.when(jnp.logical_not(any_tie))
    def _plain_bias():
        def body(j, carry):
            sc_scr[j] = jnp.where(sc_scr[j] >= lo, 0.0, MASK_BIAS)
            return carry

        lax.fori_loop(0, nch, body, 0)

    @pl.when(any_tie)
    def _tie_bias():
        need = kf - ch
        r = lax.broadcasted_iota(jnp.int32, (tk, tk), 0)
        c = lax.broadcasted_iota(jnp.int32, (tk, tk), 1)
        upto = jnp.where(c <= r, 1.0, 0.0).astype(BF16)

        def body(j, seen):
            x = sc_scr[j]
            band = jnp.where(x >= lo, jnp.where(x >= hi, 0.0, 1.0), 0.0)
            rank = _dot(upto, band.astype(BF16)) + seen
            sc_scr[j] = jnp.where(
                x >= lo, jnp.where(x >= hi, 0.0, jnp.where(rank <= need, 0.0, MASK_BIAS)), MASK_BIAS)
            return seen + colsum(band)

        lax.fori_loop(0, nch, body, jnp.zeros((1, tq), F32))

    acc_scr[...] = jnp.zeros((MIX_WIDTH, tq), F32)

    def logits_of(j):
        r0 = pl.multiple_of(j * tk, tk)
        return _dot_nt(kn_scr[pl.ds(r0, tk), :], qm_scr[...])

    def softmax_pv(j, logits, ms, ls):
        bias = sc_scr[j]
        new_ms, new_ls, alphas, pvs = [], [], [], []
        for h in range(N_HEADS):
            s = logits[:, h * tq:(h + 1) * tq] + bias
            m_new = jnp.maximum(ms[h], colmax(s))
            alphas.append(jnp.exp(ms[h] - m_new))
            p = jnp.exp(s - m_new)
            new_ls.append(alphas[h] * ls[h] + colsum(p))
            pvs.append(_dot(vt_scr[j, pl.ds(HEAD_DIM * h, HEAD_DIM), :], p.astype(BF16)))
            new_ms.append(m_new)
        for h in range(N_HEADS):
            rows = pl.ds(HEAD_DIM * h, HEAD_DIM)
            acc_scr[rows, :] = alphas[h] * acc_scr[rows, :] + pvs[h]
        return tuple(new_ms), tuple(new_ls)

    def attn_body(m, carry):
        ms, ls = carry
        js = [m * n_diag + d for d in range(n_diag)]
        logits = [logits_of(j) for j in js]
        for j, lg in zip(js, logits):
            ms, ls = softmax_pv(j, lg, ms, ls)
        return ms, ls

    m0 = tuple(jnp.full((1, tq), MAX_INIT, F32) for _ in range(N_HEADS))
    l0 = tuple(jnp.zeros((1, tq), F32) for _ in range(N_HEADS))
    _, ls = lax.fori_loop(0, i + 1, attn_body, (m0, l0))
    for h in range(N_HEADS):
        rows = pl.ds(HEAD_DIM * h, HEAD_DIM)
        acc_scr[rows, :] = acc_scr[rows, :] / ls[h]
    o_ref[...] = acc_scr[...].T


def _mixer_a(z, qg, kg, *, batch, seq):
    tq, tk = MIXA_TQ, MIXA_TK
    nq = seq // tq
    topk = min(TOPK_MAX, seq // 4)
    g0 = z.shape[1] // SEG - N_SEGS

    def qblk(c):
        return pl.BlockSpec((tq, SEG), lambda b, i, c=c: (b * nq + i, c))

    def kblk(c):
        return pl.BlockSpec((seq, SEG), lambda b, i, c=c: (b, c))

    return pl.pallas_call(
        functools.partial(_mixer_a_kernel, seq=seq, topk=topk),
        out_shape=jax.ShapeDtypeStruct((batch * seq, MIX_WIDTH), F32),
        grid=(batch, nq),
        in_specs=[
            qblk(g0 + 0), kblk(g0 + 1), kblk(g0 + 2), qblk(g0 + 3), qblk(g0 + 10), kblk(g0 + 10),
            pl.BlockSpec((1, MIX_WIDTH), lambda b, i: (0, 0)),
            pl.BlockSpec((1, MIX_WIDTH), lambda b, i: (0, 0)),
        ],
        out_specs=pl.BlockSpec((tq, MIX_WIDTH), lambda b, i: (b * nq + i, 0)),
        scratch_shapes=[
            pltpu.VMEM((seq, MIX_WIDTH), BF16),
            pltpu.VMEM((seq // tk, MIX_WIDTH, tk), BF16),
            pltpu.VMEM((seq, IDX_DIM), BF16),
            pltpu.VMEM((N_HEADS * tq, MIX_WIDTH), BF16),
            pltpu.VMEM((IDX_HEADS, tq, IDX_DIM), BF16),
            pltpu.VMEM((seq // tk, tk, tq), F32),
            pltpu.VMEM((MIX_WIDTH, tq), F32),
        ],
        compiler_params=_cparams(("parallel", "arbitrary")),
        name="mixer_a",
    )(z, z, z, z, z, z, qg, kg)


def _mixer_d_kernel(sq_ref, sk_ref, sv_ref, o_ref, k_scr, vm_scr, qm_scr, acc_scr, carry_scr, *, seq):
    t = ATTN_T
    i = pl.program_id(1)
    lane_head = _lane_head((t, MIX_WIDTH))

    @pl.when(i == 0)
    def _prep_keys():
        def body(c, carry):
            r0 = pl.multiple_of(c * t, t)
            k_scr[pl.ds(r0, t), :] = sk_ref[pl.ds(r0, t), :].astype(BF16)
            v = sv_ref[pl.ds(r0, t), :].astype(BF16)
            for h in range(N_HEADS):
                vm_scr[h, pl.ds(r0, t), :] = jnp.where(lane_head == h, v, jnp.zeros_like(v))
            return carry

        lax.fori_loop(0, seq // t, body, 0)

    q = (sq_ref[...] * ATTN_SCALE).astype(BF16)
    for h in range(N_HEADS):
        qm_scr[pl.ds(h * t, t), :] = jnp.where(lane_head == h, q, jnp.zeros_like(q))
    acc_scr[...] = jnp.zeros((t, MIX_WIDTH), F32)
    carry_scr[...] = jnp.zeros((N_HEADS, t, LANES), F32)

    row = lax.broadcasted_iota(jnp.int32, (t, t), 0)
    col = lax.broadcasted_iota(jnp.int32, (t, t), 1)
    strict = col < row
    later = jnp.where(row > col, 1.0, 0.0).astype(BF16)

    def chunk(j, masked):
        r0 = pl.multiple_of(j * t, t)
        z_all = _dot_nt(qm_scr[...], k_scr[pl.ds(r0, t), :])
        pv = None
        cmin = None
        for h in range(N_HEADS):
            z = z_all[h * t:(h + 1) * t, :]
            nl = jnp.maximum(z, 0.0) + jnp.log(1.0 + jnp.exp(-jnp.abs(z)))
            if masked:
                nl = jnp.where(strict, nl, 0.0)
            carry = carry_scr[h]
            tail = _split_dot(nl, later) + _rep2(carry)
            a = jnp.exp(z - nl - tail)
            if masked:
                a = jnp.where(strict, a, 0.0)
            d = _dot(a.astype(BF16), vm_scr[h, pl.ds(r0, t), :])
            pv = d if pv is None else pv + d
            carry = carry + jnp.sum(nl, axis=1, keepdims=True)
            carry_scr[h] = carry
            cmin = carry if cmin is None else jnp.minimum(cmin, carry)
        acc_scr[...] += pv
        return jnp.min(cmin)

    def cond(c):
        return jnp.logical_and(c[0] <= i, c[1] < SB_EXIT)

    def body(c):
        return c[0] + 1, chunk(i - c[0], False)

    lax.while_loop(cond, body, (jnp.int32(1), chunk(i, True)))
    o_ref[...] = acc_scr[...]


def _mixer_d(z, *, batch, seq):
    t = ATTN_T
    nq = seq // t
    g0 = z.shape[1] // SEG - N_SEGS

    return pl.pallas_call(
        functools.partial(_mixer_d_kernel, seq=seq),
        out_shape=jax.ShapeDtypeStruct((batch * seq, MIX_WIDTH), F32),
        grid=(batch, nq),
        in_specs=[
            pl.BlockSpec((t, SEG), lambda b, i: (b * nq + i, g0 + 7)),
            pl.BlockSpec((seq, SEG), lambda b, i: (b, g0 + 8)),
            pl.BlockSpec((seq, SEG), lambda b, i: (b, g0 + 9)),
        ],
        out_specs=pl.BlockSpec((t, MIX_WIDTH), lambda b, i: (b * nq + i, 0)),
        scratch_shapes=[
            pltpu.VMEM((seq, MIX_WIDTH), BF16),
            pltpu.VMEM((N_HEADS, seq, MIX_WIDTH), BF16),
            pltpu.VMEM((N_HEADS * t, MIX_WIDTH), BF16),
            pltpu.VMEM((t, MIX_WIDTH), F32),
            pltpu.VMEM((N_HEADS, t, LANES), F32),
        ],
        compiler_params=_cparams(("parallel", "arbitrary")),
        name="mixer_d",
    )(z, z, z)


def _merge_kernel(x_ref, oa_ref, od_ref, pin_ref, halo_ref, gu_ref, gv_ref, gl_ref,
                  poolw_ref, pscale_ref, gmn_ref, wcat_ref, gbias_ref, wbr_ref, wout_ref,
                  o_ref, ext_scr, *, seq):
    tm = MERGE_TM
    d = o_ref.shape[1]
    tile_in_seq = pl.program_id(0) % (seq // tm)

    pin = pin_ref[...]
    halo = halo_ref[...]
    ext_scr[0:POOL_HALO, :] = jnp.where(tile_in_seq == 0, jnp.zeros_like(halo), halo)
    ext_scr[POOL_HALO:POOL_HALO + tm, :] = pin
    lane_grp = lax.broadcasted_iota(jnp.int32, (1, MIX_WIDTH), 1) // (MIX_WIDTH // len(POOL_WINDOWS))
    wsum = jnp.zeros((tm, MIX_WIDTH), F32)
    run = pin
    for j in range(1, max(POOL_WINDOWS)):
        run = run + ext_scr[POOL_HALO - j:POOL_HALO - j + tm, :]
        if (j + 1) in POOL_WINDOWS:
            wsum = jnp.where(lane_grp == POOL_WINDOWS.index(j + 1), run, wsum)
    win = jnp.zeros((1, MIX_WIDTH), F32)
    for g, wlen in enumerate(POOL_WINDOWS):
        win = jnp.where(lane_grp == g, float(wlen), win)
    pos1 = (tile_in_seq * tm + lax.broadcasted_iota(jnp.int32, (tm, 1), 0) + 1).astype(F32)
    pooled = wsum / jnp.minimum(pos1, win) - pin
    o_b = _dot(pooled.astype(BF16), poolw_ref[...]) * pscale_ref[...]

    gu = _gelu_tanh(gu_ref[...])
    gv = _gelu_tanh(gv_ref[...])
    mu = jnp.mean(gv, axis=-1, keepdims=True)
    xc = gv - mu
    var = jnp.mean(xc * xc, axis=-1, keepdims=True)
    vb = (xc * lax.rsqrt(var + EPS) * gmn_ref[...]).astype(BF16)
    kw = GMLP_GROUPS * GMLP_CHUNK
    wr = lax.broadcasted_iota(jnp.int32, (GMLP_CHUNK, kw), 0)
    wc = lax.broadcasted_iota(jnp.int32, (GMLP_CHUNK, kw), 1)
    wcat = jnp.where((wc % GMLP_CHUNK) <= wr, wcat_ref[...], jnp.zeros((GMLP_CHUNK, kw), BF16))
    gr = lax.broadcasted_iota(jnp.int32, (kw, MIX_WIDTH), 0) // GMLP_CHUNK
    gc = lax.broadcasted_iota(jnp.int32, (kw, MIX_WIDTH), 1) // (MIX_WIDTH // GMLP_GROUPS)
    mixed = []
    for c in range(tm // GMLP_CHUNK):
        vc = vb[c * GMLP_CHUNK:(c + 1) * GMLP_CHUNK, :]
        vbd = jnp.where(gr == gc, jnp.concatenate([vc] * GMLP_GROUPS, axis=0),
                        jnp.zeros((kw, MIX_WIDTH), BF16))
        mixed.append(_dot(wcat, vbd) + gbias_ref[...])
    o_c = gu * jnp.concatenate(mixed, axis=0)

    merged = jnp.zeros((tm, d), F32)
    for n, o_n in enumerate((oa_ref[...], o_b, o_c, od_ref[...])):
        proj = _dot(o_n.astype(BF16), wbr_ref[n])
        merged = merged + _sigmoid(gl_ref[:, n * d:(n + 1) * d]) * proj
    o_ref[...] = x_ref[...] + _dot(merged.astype(BF16), wout_ref[...])


def _merge(x2, o_a, o_d, z, poolw_bd, pscale, gmn, wcat, gbias, wbr, wout, *, seq):
    n, d = x2.shape
    tm = MERGE_TM
    g0 = N_BRANCH * d // SEG

    def seg(c):
        return pl.BlockSpec((tm, SEG), lambda i, c=c: (i, c))

    def full(a):
        nd = a.ndim
        return pl.BlockSpec(a.shape, lambda i, nd=nd: (0,) * nd)

    halo_blocks = tm // POOL_HALO
    return pl.pallas_call(
        functools.partial(_merge_kernel, seq=seq),
        out_shape=jax.ShapeDtypeStruct((n, d), F32),
        grid=(n // tm,),
        in_specs=[
            pl.BlockSpec((tm, d), lambda i: (i, 0)),
            pl.BlockSpec((tm, MIX_WIDTH), lambda i: (i, 0)),
            pl.BlockSpec((tm, MIX_WIDTH), lambda i: (i, 0)),
            seg(g0 + 4),
            pl.BlockSpec((POOL_HALO, SEG),
                         lambda i: (jnp.maximum(i * halo_blocks - 1, 0), g0 + 4)),
            seg(g0 + 5), seg(g0 + 6),
            pl.BlockSpec((tm, N_BRANCH * d), lambda i: (i, 0)),
            full(poolw_bd), full(pscale), full(gmn), full(wcat), full(gbias), full(wbr), full(wout),
        ],
        out_specs=pl.BlockSpec((tm, d), lambda i: (i, 0)),
        scratch_shapes=[pltpu.VMEM((POOL_HALO + tm, MIX_WIDTH), F32)],
        compiler_params=_cparams(("parallel",)),
        name="merge",
    )(x2, o_a, o_d, z, z, z, z, z, poolw_bd, pscale, gmn, wcat, gbias, wbr, wout)


def _ffn_kernel(x_ref, g_ref, wg_ref, wu_ref, wd_ref, o_ref, h_scr, acc_scr):
    k = pl.program_id(1)

    @pl.when(k == 0)
    def _():
        x = x_ref[...]
        ms = jnp.mean(x * x, axis=-1, keepdims=True)
        h_scr[...] = (x * lax.rsqrt(ms + EPS) * g_ref[...]).astype(BF16)
        acc_scr[...] = jnp.zeros_like(acc_scr)

    h = h_scr[...]
    g = _dot(h, wg_ref[...])
    u = _dot(h, wu_ref[...])
    acc_scr[...] += _dot((g * _sigmoid(g) * u).astype(BF16), wd_ref[...])

    @pl.when(k == pl.num_programs(1) - 1)
    def _():
        o_ref[...] = x_ref[...] + acc_scr[...]


def _ffn(x2, gain, wg, wu, wd):
    n, d = x2.shape
    d_ff = wg.shape[1]
    return pl.pallas_call(
        _ffn_kernel,
        out_shape=jax.ShapeDtypeStruct((n, d), F32),
        grid=(n // FFN_TM, d_ff // FFN_TF),
        in_specs=[
            pl.BlockSpec((FFN_TM, d), lambda i, k: (i, 0)),
            pl.BlockSpec((1, d), lambda i, k: (0, 0)),
            pl.BlockSpec((d, FFN_TF), lambda i, k: (0, k)),
            pl.BlockSpec((d, FFN_TF), lambda i, k: (0, k)),
            pl.BlockSpec((FFN_TF, d), lambda i, k: (k, 0)),
        ],
        out_specs=pl.BlockSpec((FFN_TM, d), lambda i, k: (i, 0)),
        scratch_shapes=[pltpu.VMEM((FFN_TM, d), BF16), pltpu.VMEM((FFN_TM, d), F32)],
        compiler_params=_cparams(("parallel", "arbitrary")),
        name="ffn",
    )(x2, gain, wg, wu, wd)


def _prep_w_in(w_in_l, d):
    a_end = 3 * MIX_WIDTH + IDX_HEADS * IDX_DIM
    i_end = a_end + IDX_DIM + IDX_HEADS
    r_end = i_end + 6 * MIX_WIDTH
    pad = SEG - (IDX_DIM + IDX_HEADS)
    return jnp.concatenate(
        [w_in_l[:, r_end:], w_in_l[:, :a_end], w_in_l[:, i_end:r_end], w_in_l[:, a_end:i_end],
         jnp.zeros((d, pad), w_in_l.dtype)], axis=1).astype(BF16)


def kernel(x, mix_norm, w_in, attn_q_norm, attn_k_norm, pool_w, pool_scale, gmlp_norm, gmlp_w_s,
           gmlp_b, w_branch, w_out, ffn_norm, w_ffn_gate, w_ffn_up, w_ffn_down):
    batch, seq, d = x.shape
    depth = w_in.shape[0]
    assert seq % ATTN_T == 0 and seq % MIXA_TQ == 0 and seq % MERGE_TM == 0 and (batch * seq) % INPROJ_TM == 0
    x2 = x.reshape(batch * seq, d)
    for l in range(depth):
        w_l = _prep_w_in(w_in[l], d)
        z = _inproj(x2, mix_norm[l][None, :], w_l)
        qg = jnp.tile(attn_q_norm[l], N_HEADS)[None, :]
        kg = jnp.tile(attn_k_norm[l], N_HEADS)[None, :]
        o_a = _mixer_a(z, qg, kg, batch=batch, seq=seq)
        o_d = _mixer_d(z, batch=batch, seq=seq)
        poolw_bd = jax.scipy.linalg.block_diag(*[pool_w[l, g] for g in range(len(POOL_WINDOWS))]).astype(BF16)
        wcat = jnp.concatenate([gmlp_w_s[l, g] for g in range(GMLP_GROUPS)], axis=1).astype(BF16)
        gbias = jnp.repeat(gmlp_b[l].T, MIX_WIDTH // GMLP_GROUPS, axis=1)
        x2 = _merge(x2, o_a, o_d, z, poolw_bd, pool_scale[l][None, :], gmlp_norm[l][None, :], wcat,
                    gbias, w_branch[l].astype(BF16), w_out[l].astype(BF16), seq=seq)
        x2 = _ffn(x2, ffn_norm[l][None, :], w_ffn_gate[l].astype(BF16), w_ffn_up[l].astype(BF16),
                  w_ffn_down[l].astype(BF16))
    return x2.reshape(batch, seq, d)
```
